```python
import math
import jax, jax.numpy as jnp
from jax import lax
import numpy as np

D_MODEL = 1024
BATCH = 4
SEQ = 4096
DEPTH = 4

CHUNK = 64
HEAD_DIM = 64
N_SELF_HEADS = 12
N_DIFF_HEADS = 6
N_MEM_HEADS = 4
N_MEM = 256
SELF_WIDTH = N_SELF_HEADS * HEAD_DIM
MEM_WIDTH = N_MEM_HEADS * HEAD_DIM
IN_WIDTH = 3 * SELF_WIDTH + MEM_WIDTH
LEFT_CHUNKS = 8
BAND = (LEFT_CHUNKS + 1) * CHUNK
REL_CLIP = 128
N_REL = 2 * REL_CLIP + 1
Q_BLOCK = 128
D_FF = 2816
ROPE_THETA = 10000.0
EPS = 1e-6
NEG_INF = -1e30
N_MIXERS = 2
N_A_LAYERS = (DEPTH + 1) // 2
N_B_LAYERS = DEPTH // 2

kernel_name = 'hybrid_chunked_diff_memory_macaron'


def rmsnorm(x, g):
    x32 = x.astype(jnp.float32)
    y = x32 * lax.rsqrt(jnp.mean(x32 * x32, axis=-1, keepdims=True) + EPS)
    return (y * g.astype(jnp.float32)).astype(x.dtype)


def swiglu(h, w_gate, w_up, w_down):
    return (jax.nn.silu(h @ w_gate) * (h @ w_up)) @ w_down


def rope_tables(positions):
    inv_freq = 1.0 / (ROPE_THETA ** (jnp.arange(0, HEAD_DIM, 2, dtype=jnp.float32) / HEAD_DIM))
    ang = positions.astype(jnp.float32)[..., None] * inv_freq
    return jnp.cos(ang), jnp.sin(ang)


def apply_rope(x, cos, sin):
    x32 = x.astype(jnp.float32)
    x1, x2 = jnp.split(x32, 2, axis=-1)
    return jnp.concatenate([x1 * cos - x2 * sin, x2 * cos + x1 * sin], axis=-1).astype(x.dtype)


def chunked_rel_attention(q, k, v, rel_bias):
    B, S, H, D = q.shape
    n_chunks = S // CHUNK
    pad = ((0, 0), (LEFT_CHUNKS * CHUNK, 0), (0, 0), (0, 0))
    kp = jnp.pad(k, pad)
    vp = jnp.pad(v, pad)
    qi = jnp.arange(CHUNK)[:, None]
    ks = jnp.arange(BAND)[None, :]
    rel = LEFT_CHUNKS * CHUNK + qi - ks
    bias = rel_bias[:, jnp.clip(rel, -REL_CLIP, REL_CLIP) + REL_CLIP].astype(jnp.float32)
    scale = HEAD_DIM ** -0.5
    slot = jnp.arange(BAND)

    def one_chunk(c):
        start = c * CHUNK
        qc = lax.dynamic_slice_in_dim(q, start, CHUNK, axis=1)
        kc = lax.dynamic_slice_in_dim(kp, start, BAND, axis=1)
        vc = lax.dynamic_slice_in_dim(vp, start, BAND, axis=1)
        s = jnp.einsum('bqhd,bkhd->bhqk', qc, kc).astype(jnp.float32) * scale + bias
        valid = slot >= (LEFT_CHUNKS - c) * CHUNK
        s = jnp.where(valid, s, NEG_INF)
        p = jax.nn.softmax(s, axis=-1).astype(v.dtype)
        return jnp.einsum('bhqk,bkhd->bqhd', p, vc)

    out = lax.map(one_chunk, jnp.arange(n_chunks))
    return out.transpose(1, 0, 2, 3, 4).reshape(B, S, H, D)


def diff_attention(q, k, v, lam):
    B, S, H, _, D = q.shape
    n_blocks = S // Q_BLOCK
    scale = D ** -0.5
    k_chunk = jnp.arange(S) // CHUNK
    qb = q.reshape(B, n_blocks, Q_BLOCK, H, 2, D).transpose(1, 0, 2, 3, 4, 5)

    def one_block(args):
        q_blk, b = args
        s = jnp.einsum('bqhcd,bkhcd->bhcqk', q_blk, k).astype(jnp.float32) * scale
        q_chunk = (b * Q_BLOCK + jnp.arange(Q_BLOCK)) // CHUNK
        allowed = k_chunk[None, :] <= q_chunk[:, None]
        s = jnp.where(allowed, s, NEG_INF)
        p = jax.nn.softmax(s, axis=-1)
        a = (p[:, :, 0] - lam * p[:, :, 1]).astype(v.dtype)
        return jnp.einsum('bhqk,bkhe->bqhe', a, v)

    out = lax.map(one_block, (qb, jnp.arange(n_blocks)))
    return out.transpose(1, 0, 2, 3, 4).reshape(B, S, H, v.shape[-1])


def memory_attention(q, k, v):
    s = jnp.einsum('bshd,bnhd->bhsn', q, k).astype(jnp.float32) * (HEAD_DIM ** -0.5)
    p = jax.nn.softmax(s, axis=-1).astype(v.dtype)
    return jnp.einsum('bhsn,bnhd->bshd', p, v)


def setup_inputs(seed: int = 0) -> dict:
    key = jax.random.key(seed)
    ks = iter(jax.random.split(key, 40))
    f32 = jnp.float32

    def nrm(shape, scale):
        return jax.random.normal(next(ks), shape, f32) * scale

    def gain(shape):
        return 1.0 + 0.02 * jax.random.normal(next(ks), shape, f32)

    x = nrm((BATCH, SEQ, D_MODEL), 1.0)
    mem = nrm((BATCH, N_MEM, D_MODEL), 1.0)
    offset = jax.random.randint(next(ks), (BATCH, 1), 0, SEQ, dtype=jnp.int32)
    positions = offset + jnp.arange(SEQ, dtype=jnp.int32)[None, :]
    return {
        'x': x,
        'mem': mem,
        'positions': positions,
        'ffn1_norm': gain((DEPTH, D_MODEL)),
        'ffn1_w_gate': nrm((DEPTH, D_MODEL, D_FF), D_MODEL ** -0.5),
        'ffn1_w_up': nrm((DEPTH, D_MODEL, D_FF), D_MODEL ** -0.5),
        'ffn1_w_down': nrm((DEPTH, D_FF, D_MODEL), D_FF ** -0.5),
        'mix_norm': gain((DEPTH, D_MODEL)),
        'mem_norm': gain((DEPTH, D_MODEL)),
        'w_in': nrm((DEPTH, D_MODEL, IN_WIDTH), D_MODEL ** -0.5),
        'w_mem_kv': nrm((DEPTH, D_MODEL, 2 * MEM_WIDTH), D_MODEL ** -0.5),
        'mem_q_norm': gain((DEPTH, HEAD_DIM)),
        'mem_k_norm': gain((DEPTH, HEAD_DIM)),
        'w_out': nrm((DEPTH, SELF_WIDTH + MEM_WIDTH, D_MODEL), (SELF_WIDTH + MEM_WIDTH) ** -0.5),
        'a_q_norm': gain((N_A_LAYERS, HEAD_DIM)),
        'a_k_norm': gain((N_A_LAYERS, HEAD_DIM)),
        'a_rel_bias': nrm((N_A_LAYERS, N_SELF_HEADS, N_REL), 0.1),
        'b_q_norm': gain((N_B_LAYERS, HEAD_DIM)),
        'b_k_norm': gain((N_B_LAYERS, HEAD_DIM)),
        'b_lambda_q1': nrm((N_B_LAYERS, HEAD_DIM), 0.1),
        'b_lambda_k1': nrm((N_B_LAYERS, HEAD_DIM), 0.1),
        'b_lambda_q2': nrm((N_B_LAYERS, HEAD_DIM), 0.1),
        'b_lambda_k2': nrm((N_B_LAYERS, HEAD_DIM), 0.1),
        'b_subln': gain((N_B_LAYERS, 2 * HEAD_DIM)),
        'ffn2_norm': gain((DEPTH, D_MODEL)),
        'ffn2_w_gate': nrm((DEPTH, D_MODEL, D_FF), D_MODEL ** -0.5),
        'ffn2_w_up': nrm((DEPTH, D_MODEL, D_FF), D_MODEL ** -0.5),
        'ffn2_w_down': nrm((DEPTH, D_FF, D_MODEL), D_FF ** -0.5),
    }


def reference(x, mem, positions, ffn1_norm, ffn1_w_gate, ffn1_w_up, ffn1_w_down,
              mix_norm, mem_norm, w_in, w_mem_kv, mem_q_norm, mem_k_norm, w_out,
              a_q_norm, a_k_norm, a_rel_bias, b_q_norm, b_k_norm,
              b_lambda_q1, b_lambda_k1, b_lambda_q2, b_lambda_k2, b_subln,
              ffn2_norm, ffn2_w_gate, ffn2_w_up, ffn2_w_down):
    B, S, _ = x.shape
    cos, sin = rope_tables(positions)
    cos = cos[:, :, None, None, :]
    sin = sin[:, :, None, None, :]

    for i in range(DEPTH):
        h = rmsnorm(x, ffn1_norm[i])
        x = x + 0.5 * swiglu(h, ffn1_w_gate[i], ffn1_w_up[i], ffn1_w_down[i])

        h = rmsnorm(x, mix_norm[i])
        proj = h @ w_in[i]
        q_s, k_s, v_s, q_m = jnp.split(proj, [SELF_WIDTH, 2 * SELF_WIDTH, 3 * SELF_WIDTH], axis=-1)

        mkv = rmsnorm(mem, mem_norm[i]) @ w_mem_kv[i]
        k_m, v_m = jnp.split(mkv, 2, axis=-1)
        q_m = rmsnorm(q_m.reshape(B, S, N_MEM_HEADS, HEAD_DIM), mem_q_norm[i])
        k_m = rmsnorm(k_m.reshape(B, N_MEM, N_MEM_HEADS, HEAD_DIM), mem_k_norm[i])
        v_m = v_m.reshape(B, N_MEM, N_MEM_HEADS, HEAD_DIM)
        o_m = memory_attention(q_m, k_m, v_m).reshape(B, S, MEM_WIDTH)

        j = i // N_MIXERS
        if i % N_MIXERS == 0:
            q = rmsnorm(q_s.reshape(B, S, N_SELF_HEADS, HEAD_DIM), a_q_norm[j])
            k = rmsnorm(k_s.reshape(B, S, N_SELF_HEADS, HEAD_DIM), a_k_norm[j])
            v = v_s.reshape(B, S, N_SELF_HEADS, HEAD_DIM)
            o_s = chunked_rel_attention(q, k, v, a_rel_bias[j]).reshape(B, S, SELF_WIDTH)
        else:
            lambda_init = 0.8 - 0.6 * math.exp(-0.3 * i)
            q = rmsnorm(q_s.reshape(B, S, N_DIFF_HEADS, 2, HEAD_DIM), b_q_norm[j])
            k = rmsnorm(k_s.reshape(B, S, N_DIFF_HEADS, 2, HEAD_DIM), b_k_norm[j])
            q = apply_rope(q, cos, sin)
            k = apply_rope(k, cos, sin)
            v = v_s.reshape(B, S, N_DIFF_HEADS, 2 * HEAD_DIM)
            lam = (jnp.exp(jnp.sum(b_lambda_q1[j].astype(jnp.float32) * b_lambda_k1[j].astype(jnp.float32)))
                   - jnp.exp(jnp.sum(b_lambda_q2[j].astype(jnp.float32) * b_lambda_k2[j].astype(jnp.float32)))
                   + lambda_init)
            o = diff_attention(q, k, v, lam)
            o = rmsnorm(o, b_subln[j]) * (1.0 - lambda_init)
            o_s = o.reshape(B, S, SELF_WIDTH)

        x = x + jnp.concatenate([o_s, o_m], axis=-1) @ w_out[i]

        h = rmsnorm(x, ffn2_norm[i])
        x = x + 0.5 * swiglu(h, ffn2_w_gate[i], ffn2_w_up[i], ffn2_w_down[i])
    return x
```

```python
import functools
import math

import jax
import jax.numpy as jnp
from jax import lax
from jax.experimental import pallas as pl
from jax.experimental.pallas import tpu as pltpu

D_MODEL = 1024
DEPTH = 4
CHUNK = 64
HEAD_DIM = 64
N_SELF_HEADS = 12
N_DIFF_HEADS = 6
N_MEM_HEADS = 4
N_MEM = 256
SELF_WIDTH = N_SELF_HEADS * HEAD_DIM
MEM_WIDTH = N_MEM_HEADS * HEAD_DIM
IN_WIDTH = 3 * SELF_WIDTH + MEM_WIDTH
LEFT_CHUNKS = 8
REL_CLIP = 128
D_FF = 2816
ROPE_THETA = 10000.0
EPS = 1e-6
NEG_INF = -1e30
N_MIXERS = 2

LANES = 128
MXU_COLS = 256
VMEM_LIMIT = 56 * 1024 * 1024

FFN_ROWS = 512
PROJ_ROWS = 512
A_ROWS = 2 * CHUNK
A_BAND = A_ROWS + LEFT_CHUNKS * CHUNK
A_PAD = LEFT_CHUNKS * CHUNK
B_ROWS = 256
B_KEYS = 256

F32 = jnp.float32
BF16 = jnp.bfloat16


def _dot(a, b):
    return jnp.dot(a, b, preferred_element_type=F32)


def _dot_nt(a, b):
    return lax.dot_general(a, b, (((1,), (1,)), ((), ())), preferred_element_type=F32)


def _rms_rows(x, gain):
    ms = jnp.mean(x * x, axis=-1, keepdims=True)
    return x * lax.rsqrt(ms + EPS) * gain


def _head_rms(x, gain, seg_ones):
    x2 = x * x
    hi = x2.astype(BF16)
    lo = (x2 - hi.astype(F32)).astype(BF16)
    parts = []
    for g in range(x.shape[1] // MXU_COLS):
        sl = slice(g * MXU_COLS, (g + 1) * MXU_COLS)
        parts.append(_dot(hi[:, sl], seg_ones) + _dot(lo[:, sl], seg_ones))
    ss = parts[0] if len(parts) == 1 else jnp.concatenate(parts, axis=1)
    return x * lax.rsqrt(ss * (1.0 / HEAD_DIM) + EPS) * gain


def _rope(x, cos_t, sin_t):
    width = x.shape[1]
    reps = width // LANES
    c = jnp.concatenate([cos_t] * reps, axis=1)
    s = jnp.concatenate([sin_t] * reps, axis=1)
    half = HEAD_DIM // 2
    lane = lax.broadcasted_iota(jnp.int32, x.shape, 1)
    fwd = pltpu.roll(x, width - half, axis=1)
    bwd = pltpu.roll(x, half, axis=1)
    partner = jnp.where((lane % HEAD_DIM) < half, fwd, bwd)
    return x * c + partner * s


def _swiglu_residual(x, gain, wg_ref, wu_ref, wd_ref):
    h = _rms_rows(x, gain).astype(BF16)
    gate = _dot(h, wg_ref[...])
    up = _dot(h, wu_ref[...])
    act = (gate * (1.0 / (1.0 + jnp.exp(-gate))) * up).astype(BF16)
    return x + 0.5 * _dot(act, wd_ref[...])


def _ffn_kernel(x_ref, g_ref, wg_ref, wu_ref, wd_ref, o_ref):
    o_ref[...] = _swiglu_residual(x_ref[...], g_ref[...], wg_ref, wu_ref, wd_ref)


def _out_ffn_kernel(x_ref, os_ref, om_ref, wos_ref, wom_ref, g_ref, wg_ref, wu_ref, wd_ref, o_ref):
    x = x_ref[...] + _dot(os_ref[...], wos_ref[...]) + _dot(om_ref[...], wom_ref[...])
    o_ref[...] = _swiglu_residual(x, g_ref[...], wg_ref, wu_ref, wd_ref)


def _resident(shape):
    return pl.BlockSpec(shape, lambda i: (0,) * len(shape), pipeline_mode=pl.Buffered(1))


def _ffn_call(x, gain, wg, wu, wd, mix=None):
    tokens = x.shape[0]
    rows = pl.BlockSpec((FFN_ROWS, D_MODEL), lambda i: (i, 0))
    ffn_specs = [_resident((1, D_MODEL)), _resident((D_MODEL, D_FF)), _resident((D_MODEL, D_FF)),
                 _resident((D_FF, D_MODEL))]
    if mix is None:
        body, ins, specs, name = _ffn_kernel, (x, gain, wg, wu, wd), [rows] + ffn_specs, "ffn"
    else:
        o_s, o_m, wo_s, wo_m = mix
        body, name = _out_ffn_kernel, "out_ffn"
        ins = (x, o_s, o_m, wo_s, wo_m, gain, wg, wu, wd)
        specs = [rows,
                 pl.BlockSpec((FFN_ROWS, SELF_WIDTH), lambda i: (i, 0)),
                 pl.BlockSpec((FFN_ROWS, MEM_WIDTH), lambda i: (i, 0)),
                 _resident((SELF_WIDTH, D_MODEL)), _resident((MEM_WIDTH, D_MODEL))] + ffn_specs
    return pl.pallas_call(
        body,
        grid=(tokens // FFN_ROWS,),
        in_specs=specs,
        out_specs=rows,
        out_shape=jax.ShapeDtypeStruct((tokens, D_MODEL), F32),
        compiler_params=pltpu.CompilerParams(dimension_semantics=("arbitrary",),
                                             vmem_limit_bytes=VMEM_LIMIT),
        name=name,
    )(*ins)


def _mem_kv_kernel(mem_ref, g_ref, w_ref, gk_ref, ones_ref, k_ref, v_ref):
    h = _rms_rows(mem_ref[0], g_ref[0]).astype(BF16)
    kv = _dot(h, w_ref[0])
    k_ref[0, 0] = _head_rms(kv[:, :MEM_WIDTH], gk_ref[0], ones_ref[...]).astype(BF16)
    v_ref[0, 0] = kv[:, MEM_WIDTH:].astype(BF16)


def _mem_kv_call(mem, mem_norm, w_mem_kv, gk_tiled, seg_ones):
    batch = mem.shape[0]
    out = jax.ShapeDtypeStruct((DEPTH, batch, N_MEM, MEM_WIDTH), BF16)
    kv_spec = pl.BlockSpec((1, 1, N_MEM, MEM_WIDTH), lambda l, b: (l, b, 0, 0))
    return pl.pallas_call(
        _mem_kv_kernel,
        grid=(DEPTH, batch),
        in_specs=[pl.BlockSpec((1, N_MEM, D_MODEL), lambda l, b: (b, 0, 0)),
                  pl.BlockSpec((1, 1, D_MODEL), lambda l, b: (l, 0, 0)),
                  pl.BlockSpec((1, D_MODEL, 2 * MEM_WIDTH), lambda l, b: (l, 0, 0)),
                  pl.BlockSpec((1, 1, MEM_WIDTH), lambda l, b: (l, 0, 0)),
                  pl.BlockSpec((MXU_COLS, MXU_COLS), lambda l, b: (0, 0))],
        out_specs=[kv_spec, kv_spec],
        out_shape=[out, out],
        compiler_params=pltpu.CompilerParams(dimension_semantics=("arbitrary", "arbitrary")),
        name="mem_kv",
    )(mem, mem_norm, w_mem_kv, gk_tiled, seg_ones)


def _proj_kernel(use_rope, x_ref, g_ref, w_ref, gq_ref, gk_ref, gm_ref, ones_ref, km_ref, vm_ref,
                 *rest):
    if use_rope:
        cos_ref, sin_ref, q_ref, k_ref, v_ref, om_ref = rest
    else:
        q_ref, k_ref, v_ref, om_ref = rest
    seg_ones = ones_ref[...]
    h = _rms_rows(x_ref[...], g_ref[...]).astype(BF16)
    proj = _dot(h, w_ref[...])
    q = _head_rms(proj[:, :SELF_WIDTH], gq_ref[...], seg_ones)
    k = _head_rms(proj[:, SELF_WIDTH:2 * SELF_WIDTH], gk_ref[...], seg_ones)
    if use_rope:
        q = _rope(q, cos_ref[...], sin_ref[...])
        k = _rope(k, cos_ref[...], sin_ref[...])
    scale = HEAD_DIM ** -0.5
    q_ref[...] = (q * scale).astype(BF16)
    k_ref[...] = k.astype(BF16)
    v_ref[...] = proj[:, 2 * SELF_WIDTH:3 * SELF_WIDTH].astype(BF16)

    qm = (_head_rms(proj[:, 3 * SELF_WIDTH:], gm_ref[...], seg_ones) * scale).astype(BF16)
    km = km_ref[0, 0]
    vm = vm_ref[0, 0]
    outs = []
    for hd in range(N_MEM_HEADS):
        sl = slice(hd * HEAD_DIM, (hd + 1) * HEAD_DIM)
        s = _dot_nt(qm[:, sl], km[:, sl])
        e = jnp.exp(s - jnp.max(s, axis=-1, keepdims=True))
        inv = 1.0 / jnp.sum(e, axis=-1, keepdims=True)
        outs.append(_dot(e.astype(BF16), vm[:, sl]) * inv)
    om_ref[...] = jnp.concatenate(outs, axis=1).astype(BF16)


def _proj_call(layer, x, gain, w_in, gq, gk, gm, seg_ones, mem_k, mem_v, rope, seq):
    tokens = x.shape[0]
    per_seq = seq // PROJ_ROWS
    rows = lambda w: pl.BlockSpec((PROJ_ROWS, w), lambda i: (i, 0))
    mem_spec = pl.BlockSpec((1, 1, N_MEM, MEM_WIDTH), lambda i: (layer, i // per_seq, 0, 0))
    specs = [rows(D_MODEL), _resident((1, D_MODEL)), _resident((D_MODEL, IN_WIDTH)),
             _resident((1, SELF_WIDTH)), _resident((1, SELF_WIDTH)), _resident((1, MEM_WIDTH)),
             _resident((MXU_COLS, MXU_COLS)), mem_spec, mem_spec]
    ins = [x, gain, w_in, gq, gk, gm, seg_ones, mem_k, mem_v]
    if rope is not None:
        specs += [rows(LANES), rows(LANES)]
        ins += list(rope)
    wide = jax.ShapeDtypeStruct((tokens, SELF_WIDTH), BF16)
    return pl.pallas_call(
        functools.partial(_proj_kernel, rope is not None),
        grid=(tokens // PROJ_ROWS,),
        in_specs=specs,
        out_specs=[rows(SELF_WIDTH), rows(SELF_WIDTH), rows(SELF_WIDTH), rows(MEM_WIDTH)],
        out_shape=[wide, wide, wide, jax.ShapeDtypeStruct((tokens, MEM_WIDTH), BF16)],
        compiler_params=pltpu.CompilerParams(dimension_semantics=("arbitrary",),
                                             vmem_limit_bytes=VMEM_LIMIT),
        name="proj_rope" if rope is not None else "proj",
    )(*ins)


def _chunk_attn_kernel(q_ref, k_ref, v_ref, bias_ref, o_ref, kpad, vpad):
    j = pl.program_id(2)

    @pl.when(j == 0)
    def _():
        zeros = jnp.zeros((A_PAD, LANES), BF16)
        kpad[:A_PAD, :] = zeros
        vpad[:A_PAD, :] = zeros
        kpad[A_PAD:, :] = k_ref[0]
        vpad[A_PAD:, :] = v_ref[0]

    start = pl.multiple_of(j * A_ROWS, A_ROWS)
    kb = kpad[pl.ds(start, A_BAND), :]
    vb = vpad[pl.ds(start, A_BAND), :]
    q = q_ref[0]
    slot = lax.broadcasted_iota(jnp.int32, (A_ROWS, A_BAND), 1)
    in_stream = slot >= A_PAD - j * A_ROWS
    outs = []
    for hd in range(LANES // HEAD_DIM):
        sl = slice(hd * HEAD_DIM, (hd + 1) * HEAD_DIM)
        s = _dot_nt(q[:, sl], kb[:, sl]) + bias_ref[hd]
        s = jnp.where(in_stream, s, NEG_INF)
        e = jnp.exp(s - jnp.max(s, axis=-1, keepdims=True))
        inv = 1.0 / jnp.sum(e, axis=-1, keepdims=True)
        outs.append(_dot(e.astype(BF16), vb[:, sl]) * inv)
    o_ref[0] = jnp.concatenate(outs, axis=1).astype(BF16)


def _chunk_attn_call(q, k, v, bias):
    batch, seq, _ = q.shape
    pairs = SELF_WIDTH // LANES
    heads_per = LANES // HEAD_DIM
    q_spec = pl.BlockSpec((1, A_ROWS, LANES), lambda b, p, j: (b, j, p))
    kv_spec = pl.BlockSpec((1, seq, LANES), lambda b, p, j: (b, 0, p))
    return pl.pallas_call(
        _chunk_attn_kernel,
        grid=(batch, pairs, seq // A_ROWS),
        in_specs=[q_spec, kv_spec, kv_spec,
                  pl.BlockSpec((heads_per, A_ROWS, A_BAND), lambda b, p, j: (p, 0, 0))],
        out_specs=q_spec,
        out_shape=jax.ShapeDtypeStruct((batch, seq, SELF_WIDTH), BF16),
        scratch_shapes=[pltpu.VMEM((seq + A_PAD, LANES), BF16),
                        pltpu.VMEM((seq + A_PAD, LANES), BF16)],
        compiler_params=pltpu.CompilerParams(
            dimension_semantics=("arbitrary", "arbitrary", "arbitrary")),
        name="chunk_attn",
    )(q, k, v, bias)


def _chunk_bias_table(rel_bias):
    qi = jnp.arange(A_ROWS)[:, None]
    ks = jnp.arange(A_BAND)[None, :]
    rel = A_PAD + qi - ks
    table = rel_bias[:, jnp.clip(rel, -REL_CLIP, REL_CLIP) + REL_CLIP].astype(F32)
    q_chunk = qi // CHUNK
    k_chunk = ks // CHUNK
    allowed = (k_chunk >= q_chunk) & (k_chunk <= q_chunk + LEFT_CHUNKS)
    return jnp.where(allowed[None], table, NEG_INF)


def _diff_attn_kernel(lambda_init, q_ref, k_ref, v_ref, lq1_ref, lk1_ref, lq2_ref, lk2_ref,
                      gain_ref, o_ref):
    j = pl.program_id(2)
    q = q_ref[0]
    halves = (slice(0, HEAD_DIM), slice(HEAD_DIM, 2 * HEAD_DIM))

    def attend(carry, k_blk, v_blk, mask):
        new = []
        for c in range(2):
            m, l, acc = carry[c]
            s = _dot_nt(q[:, halves[c]], k_blk[:, halves[c]])
            if mask is not None:
                s = jnp.where(mask, s, NEG_INF)
            m_new = jnp.maximum(m, jnp.max(s, axis=-1, keepdims=True))
            alpha = jnp.exp(m - m_new)
            e = jnp.exp(s - m_new)
            l = alpha * l + jnp.sum(e, axis=-1, keepdims=True)
            acc = alpha * acc + _dot(e.astype(BF16), v_blk)
            new.append((m_new, l, acc))
        return tuple(new)

    def full_block(t, carry):
        off = pl.multiple_of(t * B_KEYS, B_KEYS)
        return attend(carry, k_ref[0, pl.ds(off, B_KEYS), :], v_ref[0, pl.ds(off, B_KEYS), :], None)

    init_one = (jnp.full((B_ROWS, 1), NEG_INF, F32), jnp.zeros((B_ROWS, 1), F32),
                jnp.zeros((B_ROWS, 2 * HEAD_DIM), F32))
    carry = lax.fori_loop(0, j * (B_ROWS // B_KEYS), full_block, (init_one, init_one))

    row_chunk = lax.broadcasted_iota(jnp.int32, (B_ROWS, B_ROWS), 0) // CHUNK
    col_chunk = lax.broadcasted_iota(jnp.int32, (B_ROWS, B_ROWS), 1) // CHUNK
    diag = pl.multiple_of(j * B_ROWS, B_ROWS)
    carry = attend(carry, k_ref[0, pl.ds(diag, B_ROWS), :], v_ref[0, pl.ds(diag, B_ROWS), :],
                   col_chunk <= row_chunk)

    lam = (jnp.exp(jnp.sum(lq1_ref[...] * lk1_ref[...], axis=-1, keepdims=True))
           - jnp.exp(jnp.sum(lq2_ref[...] * lk2_ref[...], axis=-1, keepdims=True)) + lambda_init)
    (_, l0, acc0), (_, l1, acc1) = carry
    o = acc0 * (1.0 / l0) - lam * (acc1 * (1.0 / l1))
    o_ref[0] = (_rms_rows(o, gain_ref[...]) * (1.0 - lambda_init)).astype(BF16)


def _diff_attn_call(q, k, v, lq1, lk1, lq2, lk2, gain, lambda_init):
    batch, seq, _ = q.shape
    q_spec = pl.BlockSpec((1, B_ROWS, LANES), lambda b, h, j: (b, j, h))
    kv_spec = pl.BlockSpec((1, seq, LANES), lambda b, h, j: (b, 0, h))
    vec = lambda w: pl.BlockSpec((1, w), lambda b, h, j: (0, 0))
    return pl.pallas_call(
        functools.partial(_diff_attn_kernel, lambda_init),
        grid=(batch, N_DIFF_HEADS, seq // B_ROWS),
        in_specs=[q_spec, kv_spec, kv_spec, vec(HEAD_DIM), vec(HEAD_DIM), vec(HEAD_DIM),
                  vec(HEAD_DIM), vec(2 * HEAD_DIM)],
        out_specs=q_spec,
        out_shape=jax.ShapeDtypeStruct((batch, seq, SELF_WIDTH), BF16),
        compiler_params=pltpu.CompilerParams(
            dimension_semantics=("arbitrary", "arbitrary", "arbitrary")),
        name="diff_attn",
    )(q, k, v, lq1, lk1, lq2, lk2, gain)


def _rope_tables(positions):
    inv_freq = 1.0 / (ROPE_THETA ** (jnp.arange(0, HEAD_DIM, 2, dtype=F32) / HEAD_DIM))
    ang = positions.astype(F32).reshape(-1, 1) * inv_freq
    cos, sin = jnp.cos(ang), jnp.sin(ang)
    reps = LANES // HEAD_DIM
    return (jnp.tile(jnp.concatenate([cos, cos], axis=1), (1, reps)),
            jnp.tile(jnp.concatenate([-sin, sin], axis=1), (1, reps)))


def kernel(x, mem, positions, ffn1_norm, ffn1_w_gate, ffn1_w_up, ffn1_w_down, mix_norm, mem_norm, w_in, w_mem_kv, mem_q_norm, mem_k_norm, w_out, a_q_norm, a_k_norm, a_rel_bias, b_q_norm, b_k_norm, b_lambda_q1, b_lambda_k1, b_lambda_q2, b_lambda_k2, b_subln, ffn2_norm, ffn2_w_gate, ffn2_w_up, ffn2_w_down):
    batch, seq, _ = x.shape
    tokens = batch * seq
    row = lambda a: a.reshape(1, -1).astype(F32)
    tiled = lambda a, heads: jnp.tile(a.astype(F32), heads).reshape(1, -1)
    bf = lambda a: a.astype(BF16)

    seg = jnp.arange(MXU_COLS) // HEAD_DIM
    seg_ones = (seg[:, None] == seg[None, :]).astype(BF16)
    rope = _rope_tables(positions)

    gk_mem = jnp.stack([tiled(mem_k_norm[i], N_MEM_HEADS) for i in range(DEPTH)])
    mem_k, mem_v = _mem_kv_call(mem, mem_norm.reshape(DEPTH, 1, D_MODEL), bf(w_mem_kv), gk_mem, seg_ones)

    xf = x.reshape(tokens, D_MODEL)
    for i in range(DEPTH):
        j = i // N_MIXERS
        mixer_a = i % N_MIXERS == 0
        xf = _ffn_call(xf, row(ffn1_norm[i]), bf(ffn1_w_gate[i]), bf(ffn1_w_up[i]), bf(ffn1_w_down[i]))
        gq, gk = (a_q_norm[j], a_k_norm[j]) if mixer_a else (b_q_norm[j], b_k_norm[j])
        q, k, v, o_m = _proj_call(i, xf, row(mix_norm[i]), bf(w_in[i]), tiled(gq, N_SELF_HEADS),
                                  tiled(gk, N_SELF_HEADS), tiled(mem_q_norm[i], N_MEM_HEADS),
                                  seg_ones, mem_k, mem_v, None if mixer_a else rope, seq)
        q, k, v = (a.reshape(batch, seq, SELF_WIDTH) for a in (q, k, v))
        if mixer_a:
            o_s = _chunk_attn_call(q, k, v, _chunk_bias_table(a_rel_bias[j]))
        else:
            lambda_init = 0.8 - 0.6 * math.exp(-0.3 * i)
            o_s = _diff_attn_call(q, k, v, row(b_lambda_q1[j]), row(b_lambda_k1[j]), row(b_lambda_q2[j]),
                                  row(b_lambda_k2[j]), row(b_subln[j]), lambda_init)
        w_o = bf(w_out[i])
        xf = _ffn_call(xf, row(ffn2_norm[i]), bf(ffn2_w_gate[i]), bf(ffn2_w_up[i]), bf(ffn2_w_down[i]),
                       mix=(o_s.reshape(tokens, SELF_WIDTH), o_m, w_o[:SELF_WIDTH], w_o[SELF_WIDTH:]))
    return xf.reshape(batch, seq, D_MODEL)
```

```python
import functools
import math

import jax
import jax.numpy as jnp
from jax import lax
from jax.experimental import pallas as pl
from jax.experimental.pallas import tpu as pltpu

D_MODEL = 1024
DEPTH = 4
CHUNK = 64
HEAD_DIM = 64
N_SELF_HEADS = 12
N_DIFF_HEADS = 6
N_MEM_HEADS = 4
N_MEM = 256
SELF_WIDTH = N_SELF_HEADS * HEAD_DIM
MEM_WIDTH = N_MEM_HEADS * HEAD_DIM
IN_WIDTH = 3 * SELF_WIDTH + MEM_WIDTH
LEFT_CHUNKS = 8
REL_CLIP = 128
D_FF = 2816
ROPE_THETA = 10000.0
EPS = 1e-6
NEG_INF = -1e30
N_MIXERS = 2

LANES = 128
MXU_COLS = 256
VMEM_LIMIT = 56 * 1024 * 1024

FFN_ROWS = 512
PROJ_ROWS = 512
A_ROWS = 2 * CHUNK
A_BAND = A_ROWS + LEFT_CHUNKS * CHUNK
A_PAD = LEFT_CHUNKS * CHUNK
B_ROWS = 512
B_SUB = 128

F32 = jnp.float32
BF16 = jnp.bfloat16


def _dot(a, b):
    return jnp.dot(a, b, preferred_element_type=F32)


def _dot_nt(a, b):
    return lax.dot_general(a, b, (((1,), (1,)), ((), ())), preferred_element_type=F32)


def _rms_rows(x, gain):
    ms = jnp.mean(x * x, axis=-1, keepdims=True)
    return x * lax.rsqrt(ms + EPS) * gain


def _head_rms(x, gain, seg_ones):
    x2 = x * x
    hi = x2.astype(BF16)
    lo = (x2 - hi.astype(F32)).astype(BF16)
    parts = []
    for g in range(x.shape[1] // MXU_COLS):
        sl = slice(g * MXU_COLS, (g + 1) * MXU_COLS)
        parts.append(_dot(hi[:, sl], seg_ones) + _dot(lo[:, sl], seg_ones))
    ss = parts[0] if len(parts) == 1 else jnp.concatenate(parts, axis=1)
    return x * lax.rsqrt(ss * (1.0 / HEAD_DIM) + EPS) * gain


def _rope(x, cos_t, sin_t):
    width = x.shape[1]
    reps = width // LANES
    c = jnp.concatenate([cos_t] * reps, axis=1)
    s = jnp.concatenate([sin_t] * reps, axis=1)
    half = HEAD_DIM // 2
    lane = lax.broadcasted_iota(jnp.int32, x.shape, 1)
    fwd = pltpu.roll(x, width - half, axis=1)
    bwd = pltpu.roll(x, half, axis=1)
    partner = jnp.where((lane % HEAD_DIM) < half, fwd, bwd)
    return x * c + partner * s


def _swiglu_residual(x, gain, wg_ref, wu_ref, wd_ref):
    h = _rms_rows(x, gain).astype(BF16)
    gate = _dot(h, wg_ref[...])
    up = _dot(h, wu_ref[...])
    act = (gate * (1.0 / (1.0 + jnp.exp(-gate))) * up).astype(BF16)
    return x + 0.5 * _dot(act, wd_ref[...])


def _ffn_kernel(x_ref, g_ref, wg_ref, wu_ref, wd_ref, o_ref):
    o_ref[...] = _swiglu_residual(x_ref[...], g_ref[...], wg_ref, wu_ref, wd_ref)


def _out_ffn_kernel(x_ref, os_ref, om_ref, wos_ref, wom_ref, g_ref, wg_ref, wu_ref, wd_ref, o_ref):
    x = x_ref[...] + _dot(os_ref[...], wos_ref[...]) + _dot(om_ref[...], wom_ref[...])
    o_ref[...] = _swiglu_residual(x, g_ref[...], wg_ref, wu_ref, wd_ref)


def _resident(shape):
    return pl.BlockSpec(shape, lambda i: (0,) * len(shape), pipeline_mode=pl.Buffered(1))


def _ffn_call(x, gain, wg, wu, wd, mix=None):
    tokens = x.shape[0]
    rows = pl.BlockSpec((FFN_ROWS, D_MODEL), lambda i: (i, 0))
    ffn_specs = [_resident((1, D_MODEL)), _resident((D_MODEL, D_FF)), _resident((D_MODEL, D_FF)),
                 _resident((D_FF, D_MODEL))]
    if mix is None:
        body, ins, specs, name = _ffn_kernel, (x, gain, wg, wu, wd), [rows] + ffn_specs, "ffn"
    else:
        o_s, o_m, wo_s, wo_m = mix
        body, name = _out_ffn_kernel, "out_ffn"
        ins = (x, o_s, o_m, wo_s, wo_m, gain, wg, wu, wd)
        specs = [rows,
                 pl.BlockSpec((FFN_ROWS, SELF_WIDTH), lambda i: (i, 0)),
                 pl.BlockSpec((FFN_ROWS, MEM_WIDTH), lambda i: (i, 0)),
                 _resident((SELF_WIDTH, D_MODEL)), _resident((MEM_WIDTH, D_MODEL))] + ffn_specs
    return pl.pallas_call(
        body,
        grid=(tokens // FFN_ROWS,),
        in_specs=specs,
        out_specs=rows,
        out_shape=jax.ShapeDtypeStruct((tokens, D_MODEL), F32),
        compiler_params=pltpu.CompilerParams(dimension_semantics=("arbitrary",),
                                             vmem_limit_bytes=VMEM_LIMIT),
        name=name,
    )(*ins)


def _mem_kv_kernel(mem_ref, g_ref, w_ref, gk_ref, ones_ref, k_ref, v_ref):
    h = _rms_rows(mem_ref[0], g_ref[0]).astype(BF16)
    kv = _dot(h, w_ref[0])
    k_ref[0, 0] = _head_rms(kv[:, :MEM_WIDTH], gk_ref[0], ones_ref[...]).astype(BF16)
    v_ref[0, 0] = kv[:, MEM_WIDTH:].astype(BF16)


def _mem_kv_call(mem, mem_norm, w_mem_kv, gk_tiled, seg_ones):
    batch = mem.shape[0]
    out = jax.ShapeDtypeStruct((DEPTH, batch, N_MEM, MEM_WIDTH), BF16)
    kv_spec = pl.BlockSpec((1, 1, N_MEM, MEM_WIDTH), lambda l, b: (l, b, 0, 0))
    return pl.pallas_call(
        _mem_kv_kernel,
        grid=(DEPTH, batch),
        in_specs=[pl.BlockSpec((1, N_MEM, D_MODEL), lambda l, b: (b, 0, 0)),
                  pl.BlockSpec((1, 1, D_MODEL), lambda l, b: (l, 0, 0)),
                  pl.BlockSpec((1, D_MODEL, 2 * MEM_WIDTH), lambda l, b: (l, 0, 0)),
                  pl.BlockSpec((1, 1, MEM_WIDTH), lambda l, b: (l, 0, 0)),
                  pl.BlockSpec((MXU_COLS, MXU_COLS), lambda l, b: (0, 0))],
        out_specs=[kv_spec, kv_spec],
        out_shape=[out, out],
        compiler_params=pltpu.CompilerParams(dimension_semantics=("arbitrary", "arbitrary")),
        name="mem_kv",
    )(mem, mem_norm, w_mem_kv, gk_tiled, seg_ones)


def _proj_kernel(use_rope, x_ref, g_ref, w_ref, gq_ref, gk_ref, gm_ref, ones_ref, km_ref, vm_ref,
                 *rest):
    if use_rope:
        cos_ref, sin_ref, q_ref, k_ref, v_ref, om_ref = rest
    else:
        q_ref, k_ref, v_ref, om_ref = rest
    seg_ones = ones_ref[...]
    h = _rms_rows(x_ref[...], g_ref[...]).astype(BF16)
    proj = _dot(h, w_ref[...])
    q = _head_rms(proj[:, :SELF_WIDTH], gq_ref[...], seg_ones)
    k = _head_rms(proj[:, SELF_WIDTH:2 * SELF_WIDTH], gk_ref[...], seg_ones)
    if use_rope:
        q = _rope(q, cos_ref[...], sin_ref[...])
        k = _rope(k, cos_ref[...], sin_ref[...])
    scale = HEAD_DIM ** -0.5
    q_ref[...] = (q * scale).astype(BF16)
    k_ref[...] = k.astype(BF16)
    v_ref[...] = proj[:, 2 * SELF_WIDTH:3 * SELF_WIDTH].astype(BF16)

    qm = (_head_rms(proj[:, 3 * SELF_WIDTH:], gm_ref[...], seg_ones) * scale).astype(BF16)
    km = km_ref[0, 0]
    vm = vm_ref[0, 0]
    outs = []
    for hd in range(N_MEM_HEADS):
        sl = slice(hd * HEAD_DIM, (hd + 1) * HEAD_DIM)
        s = _dot_nt(qm[:, sl], km[:, sl])
        e = jnp.exp(s - jnp.max(s, axis=-1, keepdims=True))
        inv = 1.0 / jnp.sum(e, axis=-1, keepdims=True)
        outs.append(_dot(e.astype(BF16), vm[:, sl]) * inv)
    om_ref[...] = jnp.concatenate(outs, axis=1).astype(BF16)


def _proj_call(layer, x, gain, w_in, gq, gk, gm, seg_ones, mem_k, mem_v, rope, seq):
    tokens = x.shape[0]
    per_seq = seq // PROJ_ROWS
    rows = lambda w: pl.BlockSpec((PROJ_ROWS, w), lambda i: (i, 0))
    mem_spec = pl.BlockSpec((1, 1, N_MEM, MEM_WIDTH), lambda i: (layer, i // per_seq, 0, 0))
    specs = [rows(D_MODEL), _resident((1, D_MODEL)), _resident((D_MODEL, IN_WIDTH)),
             _resident((1, SELF_WIDTH)), _resident((1, SELF_WIDTH)), _resident((1, MEM_WIDTH)),
             _resident((MXU_COLS, MXU_COLS)), mem_spec, mem_spec]
    ins = [x, gain, w_in, gq, gk, gm, seg_ones, mem_k, mem_v]
    if rope is not None:
        specs += [rows(LANES), rows(LANES)]
        ins += list(rope)
    wide = jax.ShapeDtypeStruct((tokens, SELF_WIDTH), BF16)
    return pl.pallas_call(
        functools.partial(_proj_kernel, rope is not None),
        grid=(tokens // PROJ_ROWS,),
        in_specs=specs,
        out_specs=[rows(SELF_WIDTH), rows(SELF_WIDTH), rows(SELF_WIDTH), rows(MEM_WIDTH)],
        out_shape=[wide, wide, wide, jax.ShapeDtypeStruct((tokens, MEM_WIDTH), BF16)],
        compiler_params=pltpu.CompilerParams(dimension_semantics=("arbitrary",),
                                             vmem_limit_bytes=VMEM_LIMIT),
        name="proj_rope" if rope is not None else "proj",
    )(*ins)


def _chunk_attn_kernel(q_ref, k_ref, v_ref, bias_ref, o_ref, kpad, vpad):
    j = pl.program_id(2)

    @pl.when(j == 0)
    def _():
        zeros = jnp.zeros((A_PAD, LANES), BF16)
        kpad[:A_PAD, :] = zeros
        vpad[:A_PAD, :] = zeros
        kpad[A_PAD:, :] = k_ref[0]
        vpad[A_PAD:, :] = v_ref[0]

    start = pl.multiple_of(j * A_ROWS, A_ROWS)
    kb = kpad[pl.ds(start, A_BAND), :]
    vb = vpad[pl.ds(start, A_BAND), :]
    q = q_ref[0]
    slot = lax.broadcasted_iota(jnp.int32, (A_ROWS, A_BAND), 1)
    in_stream = slot >= A_PAD - j * A_ROWS
    outs = []
    for hd in range(LANES // HEAD_DIM):
        sl = slice(hd * HEAD_DIM, (hd + 1) * HEAD_DIM)
        s = _dot_nt(q[:, sl], kb[:, sl]) + bias_ref[hd]
        s = jnp.where(in_stream, s, NEG_INF)
        e = jnp.exp(s - jnp.max(s, axis=-1, keepdims=True))
        inv = 1.0 / jnp.sum(e, axis=-1, keepdims=True)
        outs.append(_dot(e.astype(BF16), vb[:, sl]) * inv)
    o_ref[0] = jnp.concatenate(outs, axis=1).astype(BF16)


def _chunk_attn_call(q, k, v, bias):
    batch, seq, _ = q.shape
    pairs = SELF_WIDTH // LANES
    heads_per = LANES // HEAD_DIM
    q_spec = pl.BlockSpec((1, A_ROWS, LANES), lambda b, p, j: (b, j, p))
    kv_spec = pl.BlockSpec((1, seq, LANES), lambda b, p, j: (b, 0, p))
    return pl.pallas_call(
        _chunk_attn_kernel,
        grid=(batch, pairs, seq // A_ROWS),
        in_specs=[q_spec, kv_spec, kv_spec,
                  pl.BlockSpec((heads_per, A_ROWS, A_BAND), lambda b, p, j: (p, 0, 0))],
        out_specs=q_spec,
        out_shape=jax.ShapeDtypeStruct((batch, seq, SELF_WIDTH), BF16),
        scratch_shapes=[pltpu.VMEM((seq + A_PAD, LANES), BF16),
                        pltpu.VMEM((seq + A_PAD, LANES), BF16)],
        compiler_params=pltpu.CompilerParams(
            dimension_semantics=("arbitrary", "arbitrary", "arbitrary")),
        name="chunk_attn",
    )(q, k, v, bias)


def _chunk_bias_table(rel_bias):
    qi = jnp.arange(A_ROWS)[:, None]
    ks = jnp.arange(A_BAND)[None, :]
    rel = A_PAD + qi - ks
    table = rel_bias[:, jnp.clip(rel, -REL_CLIP, REL_CLIP) + REL_CLIP].astype(F32)
    q_chunk = qi // CHUNK
    k_chunk = ks // CHUNK
    allowed = (k_chunk >= q_chunk) & (k_chunk <= q_chunk + LEFT_CHUNKS)
    return jnp.where(allowed[None], table, NEG_INF)


def _diff_attn_kernel(lambda_init, q_ref, k_ref, v_ref, lq1_ref, lk1_ref, lq2_ref, lk2_ref,
                      gain_ref, o_ref, qs_ref, s_a, s_b, p_a, p_b, m_ref, l_ref, acc_ref):
    j = pl.program_id(2)
    rows = 2 * B_ROWS
    q = q_ref[0]
    lane = lax.broadcasted_iota(jnp.int32, q.shape, 1)
    qs_ref[:B_ROWS, :] = jnp.where(lane < HEAD_DIM, q, jnp.zeros_like(q))
    qs_ref[B_ROWS:, :] = jnp.where(lane >= HEAD_DIM, q, jnp.zeros_like(q))
    m_ref[...] = jnp.full(m_ref.shape, NEG_INF, F32)
    l_ref[...] = jnp.zeros(l_ref.shape, F32)
    acc_ref[...] = jnp.zeros(acc_ref.shape, F32)

    def key_rows(t):
        return pl.ds(pl.multiple_of(t * B_ROWS, B_ROWS), B_ROWS)

    def scores(t, s_ref):
        s_ref[...] = _dot_nt(qs_ref[...], k_ref[0, key_rows(t), :])

    def update(t, s_ref, p_ref, diagonal):
        for c in range(rows // B_SUB):
            r = slice(c * B_SUB, (c + 1) * B_SUB)
            s = s_ref[r, :]
            if diagonal:
                base = (c * B_SUB) % B_ROWS
                row_chunk = (lax.broadcasted_iota(jnp.int32, s.shape, 0) + base) // CHUNK
                col_chunk = lax.broadcasted_iota(jnp.int32, s.shape, 1) // CHUNK
                s = jnp.where(col_chunk <= row_chunk, s, NEG_INF)
            m_old = m_ref[r, :]
            m_new = jnp.maximum(m_old, jnp.max(s, axis=-1, keepdims=True))
            alpha = jnp.exp(m_old - m_new)
            e = jnp.exp(s - m_new)
            part = e[:, :LANES]
            for g in range(1, B_ROWS // LANES):
                part = part + e[:, g * LANES:(g + 1) * LANES]
            m_ref[r, :] = m_new
            l_ref[r, :] = alpha * l_ref[r, :] + part
            acc_ref[r, :] = alpha * acc_ref[r, :]
            p_ref[r, :] = e.astype(BF16)
        acc_ref[...] += _dot(p_ref[...], v_ref[0, key_rows(t), :])

    n_full = j
    scores(0, s_a)

    def two_blocks(u, carry):
        t = 2 * u
        scores(t + 1, s_b)
        update(t, s_a, p_a, False)
        scores(t + 2, s_a)
        update(t + 1, s_b, p_b, False)
        return carry

    lax.fori_loop(0, n_full // 2, two_blocks, 0)

    @pl.when(n_full % 2 == 1)
    def _():
        scores(n_full, s_b)
        update(n_full - 1, s_a, p_a, False)
        update(n_full, s_b, p_b, True)

    @pl.when(n_full % 2 == 0)
    def _():
        update(n_full, s_a, p_a, True)

    lam = (jnp.exp(jnp.sum(lq1_ref[...] * lk1_ref[...], axis=-1, keepdims=True))
           - jnp.exp(jnp.sum(lq2_ref[...] * lk2_ref[...], axis=-1, keepdims=True)) + lambda_init)
    inv_l = 1.0 / jnp.sum(l_ref[...], axis=-1, keepdims=True)
    o_all = acc_ref[...] * inv_l
    o = o_all[:B_ROWS] - lam * o_all[B_ROWS:]
    o_ref[0] = (_rms_rows(o, gain_ref[...]) * (1.0 - lambda_init)).astype(BF16)


def _diff_attn_call(q, k, v, lq1, lk1, lq2, lk2, gain, lambda_init):
    batch, seq, _ = q.shape
    rows = 2 * B_ROWS
    q_spec = pl.BlockSpec((1, B_ROWS, LANES), lambda b, h, j: (b, j, h))
    kv_spec = pl.BlockSpec((1, seq, LANES), lambda b, h, j: (b, 0, h))
    vec = lambda w: pl.BlockSpec((1, w), lambda b, h, j: (0, 0))
    return pl.pallas_call(
        functools.partial(_diff_attn_kernel, lambda_init),
        grid=(batch, N_DIFF_HEADS, seq // B_ROWS),
        in_specs=[q_spec, kv_spec, kv_spec, vec(HEAD_DIM), vec(HEAD_DIM), vec(HEAD_DIM),
                  vec(HEAD_DIM), vec(2 * HEAD_DIM)],
        out_specs=q_spec,
        out_shape=jax.ShapeDtypeStruct((batch, seq, SELF_WIDTH), BF16),
        scratch_shapes=[pltpu.VMEM((rows, LANES), BF16),
                        pltpu.VMEM((rows, B_ROWS), F32), pltpu.VMEM((rows, B_ROWS), F32),
                        pltpu.VMEM((rows, B_ROWS), BF16), pltpu.VMEM((rows, B_ROWS), BF16),
                        pltpu.VMEM((rows, 1), F32), pltpu.VMEM((rows, LANES), F32),
                        pltpu.VMEM((rows, LANES), F32)],
        compiler_params=pltpu.CompilerParams(
            dimension_semantics=("arbitrary", "arbitrary", "arbitrary")),
        name="diff_attn",
    )(q, k, v, lq1, lk1, lq2, lk2, gain)


def _rope_tables(positions):
    inv_freq = 1.0 / (ROPE_THETA ** (jnp.arange(0, HEAD_DIM, 2, dtype=F32) / HEAD_DIM))
    ang = positions.astype(F32).reshape(-1, 1) * inv_freq
    cos, sin = jnp.cos(ang), jnp.sin(ang)
    reps = LANES // HEAD_DIM
    return (jnp.tile(jnp.concatenate([cos, cos], axis=1), (1, reps)),
            jnp.tile(jnp.concatenate([-sin, sin], axis=1), (1, reps)))


def kernel(x, mem, positions, ffn1_norm, ffn1_w_gate, ffn1_w_up, ffn1_w_down, mix_norm, mem_norm, w_in, w_mem_kv, mem_q_norm, mem_k_norm, w_out, a_q_norm, a_k_norm, a_rel_bias, b_q_norm, b_k_norm, b_lambda_q1, b_lambda_k1, b_lambda_q2, b_lambda_k2, b_subln, ffn2_norm, ffn2_w_gate, ffn2_w_up, ffn2_w_down):
    batch, seq, _ = x.shape
    tokens = batch * seq
    row = lambda a: a.reshape(1, -1).astype(F32)
    tiled = lambda a, heads: jnp.tile(a.astype(F32), heads).reshape(1, -1)
    bf = lambda a: a.astype(BF16)

    seg = jnp.arange(MXU_COLS) // HEAD_DIM
    seg_ones = (seg[:, None] == seg[None, :]).astype(BF16)
    rope = _rope_tables(positions)

    gk_mem = jnp.stack([tiled(mem_k_norm[i], N_MEM_HEADS) for i in range(DEPTH)])
    mem_k, mem_v = _mem_kv_call(mem, mem_norm.reshape(DEPTH, 1, D_MODEL), bf(w_mem_kv), gk_mem, seg_ones)

    xf = x.reshape(tokens, D_MODEL)
    for i in range(DEPTH):
        j = i // N_MIXERS
        mixer_a = i % N_MIXERS == 0
        xf = _ffn_call(xf, row(ffn1_norm[i]), bf(ffn1_w_gate[i]), bf(ffn1_w_up[i]), bf(ffn1_w_down[i]))
        gq, gk = (a_q_norm[j], a_k_norm[j]) if mixer_a else (b_q_norm[j], b_k_norm[j])
        q, k, v, o_m = _proj_call(i, xf, row(mix_norm[i]), bf(w_in[i]), tiled(gq, N_SELF_HEADS),
                                  tiled(gk, N_SELF_HEADS), tiled(mem_q_norm[i], N_MEM_HEADS),
                                  seg_ones, mem_k, mem_v, None if mixer_a else rope, seq)
        q, k, v = (a.reshape(batch, seq, SELF_WIDTH) for a in (q, k, v))
        if mixer_a:
            o_s = _chunk_attn_call(q, k, v, _chunk_bias_table(a_rel_bias[j]))
        else:
            lambda_init = 0.8 - 0.6 * math.exp(-0.3 * i)
            o_s = _diff_attn_call(q, k, v, row(b_lambda_q1[j]), row(b_lambda_k1[j]), row(b_lambda_q2[j]),
                                  row(b_lambda_k2[j]), row(b_subln[j]), lambda_init)
        w_o = bf(w_out[i])
        xf = _ffn_call(xf, row(ffn2_norm[i]), bf(ffn2_w_gate[i]), bf(ffn2_w_up[i]), bf(ffn2_w_down[i]),
                       mix=(o_s.reshape(tokens, SELF_WIDTH), o_m, w_o[:SELF_WIDTH], w_o[SELF_WIDTH:]))
    return xf.reshape(batch, seq, D_MODEL)
```

```python
import functools
import math

import jax
import jax.numpy as jnp
from jax import lax
from jax.experimental import pallas as pl
from jax.experimental.pallas import tpu as pltpu

D_MODEL = 1024
DEPTH = 4
CHUNK = 64
HEAD_DIM = 64
N_SELF_HEADS = 12
N_DIFF_HEADS = 6
N_MEM_HEADS = 4
N_MEM = 256
SELF_WIDTH = N_SELF_HEADS * HEAD_DIM
MEM_WIDTH = N_MEM_HEADS * HEAD_DIM
IN_WIDTH = 3 * SELF_WIDTH + MEM_WIDTH
LEFT_CHUNKS = 8
REL_CLIP = 128
D_FF = 2816
ROPE_THETA = 10000.0
EPS = 1e-6
NEG_INF = -1e30
N_MIXERS = 2

LANES = 128
MXU_COLS = 256
VMEM_LIMIT = 56 * 1024 * 1024

FFN_ROWS = 512
PROJ_ROWS = 512
A_ROWS = 4 * CHUNK
A_PAD = LEFT_CHUNKS * CHUNK
A_BAND = A_ROWS + A_PAD
A_RAMP = A_BAND + A_ROWS
A_SUB = 128
B_ROWS = 512
B_SUB = 128

F32 = jnp.float32
BF16 = jnp.bfloat16


def _dot(a, b):
    return jnp.dot(a, b, preferred_element_type=F32)


def _dot_nt(a, b):
    return lax.dot_general(a, b, (((1,), (1,)), ((), ())), preferred_element_type=F32)


def _rms_rows(x, gain):
    ms = jnp.mean(x * x, axis=-1, keepdims=True)
    return x * lax.rsqrt(ms + EPS) * gain


def _head_rms(x, gain, seg_ones):
    x2 = x * x
    hi = x2.astype(BF16)
    lo = (x2 - hi.astype(F32)).astype(BF16)
    parts = []
    for g in range(x.shape[1] // MXU_COLS):
        sl = slice(g * MXU_COLS, (g + 1) * MXU_COLS)
        parts.append(_dot(hi[:, sl], seg_ones) + _dot(lo[:, sl], seg_ones))
    ss = parts[0] if len(parts) == 1 else jnp.concatenate(parts, axis=1)
    return x * lax.rsqrt(ss * (1.0 / HEAD_DIM) + EPS) * gain


def _rope(x, cos_t, sin_t):
    width = x.shape[1]
    reps = width // LANES
    c = jnp.concatenate([cos_t] * reps, axis=1)
    s = jnp.concatenate([sin_t] * reps, axis=1)
    half = HEAD_DIM // 2
    lane = lax.broadcasted_iota(jnp.int32, x.shape, 1)
    fwd = pltpu.roll(x, width - half, axis=1)
    bwd = pltpu.roll(x, half, axis=1)
    partner = jnp.where((lane % HEAD_DIM) < half, fwd, bwd)
    return x * c + partner * s


def _swiglu_residual(x, gain, wg_ref, wu_ref, wd_ref):
    h = _rms_rows(x, gain).astype(BF16)
    gate = _dot(h, wg_ref[...])
    up = _dot(h, wu_ref[...])
    act = (gate * (1.0 / (1.0 + jnp.exp(-gate))) * up).astype(BF16)
    return x + 0.5 * _dot(act, wd_ref[...])


def _ffn_kernel(x_ref, g_ref, wg_ref, wu_ref, wd_ref, o_ref):
    o_ref[...] = _swiglu_residual(x_ref[...], g_ref[...], wg_ref, wu_ref, wd_ref)


def _out_ffn_kernel(x_ref, os_ref, om_ref, wos_ref, wom_ref, g_ref, wg_ref, wu_ref, wd_ref, o_ref):
    x = x_ref[...] + _dot(os_ref[...], wos_ref[...]) + _dot(om_ref[...], wom_ref[...])
    o_ref[...] = _swiglu_residual(x, g_ref[...], wg_ref, wu_ref, wd_ref)


def _resident(shape):
    return pl.BlockSpec(shape, lambda i: (0,) * len(shape), pipeline_mode=pl.Buffered(1))


def _ffn_call(x, gain, wg, wu, wd, mix=None):
    tokens = x.shape[0]
    rows = pl.BlockSpec((FFN_ROWS, D_MODEL), lambda i: (i, 0))
    ffn_specs = [_resident((1, D_MODEL)), _resident((D_MODEL, D_FF)), _resident((D_MODEL, D_FF)),
                 _resident((D_FF, D_MODEL))]
    if mix is None:
        body, ins, specs, name = _ffn_kernel, (x, gain, wg, wu, wd), [rows] + ffn_specs, "ffn"
    else:
        o_s, o_m, wo_s, wo_m = mix
        body, name = _out_ffn_kernel, "out_ffn"
        ins = (x, o_s, o_m, wo_s, wo_m, gain, wg, wu, wd)
        specs = [rows,
                 pl.BlockSpec((FFN_ROWS, SELF_WIDTH), lambda i: (i, 0)),
                 pl.BlockSpec((FFN_ROWS, MEM_WIDTH), lambda i: (i, 0)),
                 _resident((SELF_WIDTH, D_MODEL)), _resident((MEM_WIDTH, D_MODEL))] + ffn_specs
    return pl.pallas_call(
        body,
        grid=(tokens // FFN_ROWS,),
        in_specs=specs,
        out_specs=rows,
        out_shape=jax.ShapeDtypeStruct((tokens, D_MODEL), F32),
        compiler_params=pltpu.CompilerParams(dimension_semantics=("arbitrary",),
                                             vmem_limit_bytes=VMEM_LIMIT),
        name=name,
    )(*ins)


def _mem_kv_kernel(mem_ref, g_ref, w_ref, gk_ref, ones_ref, k_ref, v_ref):
    h = _rms_rows(mem_ref[0], g_ref[0]).astype(BF16)
    kv = _dot(h, w_ref[0])
    k_ref[0, 0] = _head_rms(kv[:, :MEM_WIDTH], gk_ref[0], ones_ref[...]).astype(BF16)
    v_ref[0, 0] = kv[:, MEM_WIDTH:].astype(BF16)


def _mem_kv_call(mem, mem_norm, w_mem_kv, gk_tiled, seg_ones):
    batch = mem.shape[0]
    out = jax.ShapeDtypeStruct((DEPTH, batch, N_MEM, MEM_WIDTH), BF16)
    kv_spec = pl.BlockSpec((1, 1, N_MEM, MEM_WIDTH), lambda l, b: (l, b, 0, 0))
    return pl.pallas_call(
        _mem_kv_kernel,
        grid=(DEPTH, batch),
        in_specs=[pl.BlockSpec((1, N_MEM, D_MODEL), lambda l, b: (b, 0, 0)),
                  pl.BlockSpec((1, 1, D_MODEL), lambda l, b: (l, 0, 0)),
                  pl.BlockSpec((1, D_MODEL, 2 * MEM_WIDTH), lambda l, b: (l, 0, 0)),
                  pl.BlockSpec((1, 1, MEM_WIDTH), lambda l, b: (l, 0, 0)),
                  pl.BlockSpec((MXU_COLS, MXU_COLS), lambda l, b: (0, 0))],
        out_specs=[kv_spec, kv_spec],
        out_shape=[out, out],
        compiler_params=pltpu.CompilerParams(dimension_semantics=("arbitrary", "arbitrary")),
        name="mem_kv",
    )(mem, mem_norm, w_mem_kv, gk_tiled, seg_ones)


def _proj_kernel(use_rope, x_ref, g_ref, w_ref, gq_ref, gk_ref, gm_ref, ones_ref, km_ref, vm_ref,
                 *rest):
    if use_rope:
        cos_ref, sin_ref, q_ref, k_ref, v_ref, om_ref = rest
    else:
        q_ref, k_ref, v_ref, om_ref = rest
    seg_ones = ones_ref[...]
    h = _rms_rows(x_ref[...], g_ref[...]).astype(BF16)
    proj = _dot(h, w_ref[...])
    q = _head_rms(proj[:, :SELF_WIDTH], gq_ref[...], seg_ones)
    k = _head_rms(proj[:, SELF_WIDTH:2 * SELF_WIDTH], gk_ref[...], seg_ones)
    if use_rope:
        q = _rope(q, cos_ref[...], sin_ref[...])
        k = _rope(k, cos_ref[...], sin_ref[...])
    scale = HEAD_DIM ** -0.5
    q_ref[...] = (q * scale).astype(BF16)
    k_ref[...] = k.astype(BF16)
    v_ref[...] = proj[:, 2 * SELF_WIDTH:3 * SELF_WIDTH].astype(BF16)

    qm = (_head_rms(proj[:, 3 * SELF_WIDTH:], gm_ref[...], seg_ones) * scale).astype(BF16)
    km = km_ref[0, 0]
    vm = vm_ref[0, 0]
    outs = []
    for hd in range(N_MEM_HEADS):
        sl = slice(hd * HEAD_DIM, (hd + 1) * HEAD_DIM)
        s = _dot_nt(qm[:, sl], km[:, sl])
        e = jnp.exp(s - jnp.max(s, axis=-1, keepdims=True))
        inv = 1.0 / jnp.sum(e, axis=-1, keepdims=True)
        outs.append(_dot(e.astype(BF16), vm[:, sl]) * inv)
    om_ref[...] = jnp.concatenate(outs, axis=1).astype(BF16)


def _proj_call(layer, x, gain, w_in, gq, gk, gm, seg_ones, mem_k, mem_v, rope, seq):
    tokens = x.shape[0]
    per_seq = seq // PROJ_ROWS
    rows = lambda w: pl.BlockSpec((PROJ_ROWS, w), lambda i: (i, 0))
    mem_spec = pl.BlockSpec((1, 1, N_MEM, MEM_WIDTH), lambda i: (layer, i // per_seq, 0, 0))
    specs = [rows(D_MODEL), _resident((1, D_MODEL)), _resident((D_MODEL, IN_WIDTH)),
             _resident((1, SELF_WIDTH)), _resident((1, SELF_WIDTH)), _resident((1, MEM_WIDTH)),
             _resident((MXU_COLS, MXU_COLS)), mem_spec, mem_spec]
    ins = [x, gain, w_in, gq, gk, gm, seg_ones, mem_k, mem_v]
    if rope is not None:
        specs += [rows(LANES), rows(LANES)]
        ins += list(rope)
    wide = jax.ShapeDtypeStruct((tokens, SELF_WIDTH), BF16)
    return pl.pallas_call(
        functools.partial(_proj_kernel, rope is not None),
        grid=(tokens // PROJ_ROWS,),
        in_specs=specs,
        out_specs=[rows(SELF_WIDTH), rows(SELF_WIDTH), rows(SELF_WIDTH), rows(MEM_WIDTH)],
        out_shape=[wide, wide, wide, jax.ShapeDtypeStruct((tokens, MEM_WIDTH), BF16)],
        compiler_params=pltpu.CompilerParams(dimension_semantics=("arbitrary",),
                                             vmem_limit_bytes=VMEM_LIMIT),
        name="proj_rope" if rope is not None else "proj",
    )(*ins)


def _chunk_attn_kernel(q_ref, k_ref, v_ref, ramp_ref, o_ref, tab_ref, kpad, vpad, s_a, s_b, p_a, p_b):
    seq = q_ref.shape[1]
    n_blocks = seq // A_ROWS
    n_head = A_PAD // A_ROWS
    rows = 2 * A_ROWS

    row_chunk = lax.broadcasted_iota(jnp.int32, (A_ROWS, A_BAND), 0) // CHUNK
    col_chunk = lax.broadcasted_iota(jnp.int32, (A_ROWS, A_BAND), 1) // CHUNK
    allowed = (col_chunk >= row_chunk) & (col_chunk <= row_chunk + LEFT_CHUNKS)
    for hd in range(LANES // HEAD_DIM):
        ramp = jnp.broadcast_to(ramp_ref[hd], (A_ROWS, A_RAMP))
        toeplitz = pltpu.roll(ramp, A_RAMP - A_ROWS, axis=1, stride=1, stride_axis=0)
        tab_ref[hd * A_ROWS:(hd + 1) * A_ROWS, :] = jnp.where(allowed, toeplitz[:, :A_BAND], NEG_INF)

    kpad[:A_PAD, :] = jnp.zeros((A_PAD, LANES), BF16)
    vpad[:A_PAD, :] = jnp.zeros((A_PAD, LANES), BF16)
    kpad[A_PAD:, :] = k_ref[0, :n_head * A_ROWS, :]
    vpad[A_PAD:, :] = v_ref[0, :n_head * A_ROWS, :]

    def band(ref, pad_ref, t):
        if isinstance(t, int) and t < n_head:
            return pad_ref[t * A_ROWS:t * A_ROWS + A_BAND, :]
        return ref[0, pl.ds(pl.multiple_of((t - n_head) * A_ROWS, A_ROWS), A_BAND), :]

    def block_rows(t):
        return pl.ds(pl.multiple_of(t * A_ROWS, A_ROWS), A_ROWS)

    lane = lax.broadcasted_iota(jnp.int32, (A_ROWS, LANES), 1)

    def scores(t, s_ref):
        q = q_ref[0, block_rows(t), :]
        stacked = jnp.concatenate([jnp.where(lane < HEAD_DIM, q, jnp.zeros_like(q)),
                                   jnp.where(lane >= HEAD_DIM, q, jnp.zeros_like(q))], axis=0)
        s_ref[...] = _dot_nt(stacked, band(k_ref, kpad, t))

    def update(t, s_ref, p_ref, first_slot):
        inv = []
        for c in range(rows // A_SUB):
            r = slice(c * A_SUB, (c + 1) * A_SUB)
            s = s_ref[r, :] + tab_ref[r, :]
            if first_slot:
                slot = lax.broadcasted_iota(jnp.int32, s.shape, 1)
                s = jnp.where(slot >= first_slot, s, NEG_INF)
            e = jnp.exp(s - jnp.max(s, axis=-1, keepdims=True))
            part = e[:, :LANES]
            for g in range(1, A_BAND // LANES):
                part = part + e[:, g * LANES:(g + 1) * LANES]
            inv.append(1.0 / jnp.sum(part, axis=-1, keepdims=True))
            p_ref[r, :] = e.astype(BF16)
        pv = _dot(p_ref[...], band(v_ref, vpad, t)) * jnp.concatenate(inv, axis=0)
        o_ref[0, block_rows(t), :] = jnp.where(lane < HEAD_DIM, pv[:A_ROWS], pv[A_ROWS:]).astype(BF16)

    scores(0, s_a)
    scores(1, s_b)
    update(0, s_a, p_a, A_PAD)
    scores(2, s_a)
    update(1, s_b, p_b, A_PAD - A_ROWS)

    def two_blocks(u, carry):
        t = 2 * u
        scores(t + 1, s_b)
        update(t, s_a, p_a, 0)
        scores(jnp.minimum(t + 2, n_blocks - 1), s_a)
        update(t + 1, s_b, p_b, 0)
        return carry

    lax.fori_loop(1, n_blocks // 2, two_blocks, 0)


def _chunk_attn_call(q, k, v, ramp):
    batch, seq, _ = q.shape
    rows = 2 * A_ROWS
    heads_per = LANES // HEAD_DIM
    n_head = A_PAD // A_ROWS
    seq_spec = pl.BlockSpec((1, seq, LANES), lambda b, p: (b, 0, p))
    return pl.pallas_call(
        _chunk_attn_kernel,
        grid=(batch, SELF_WIDTH // LANES),
        in_specs=[seq_spec, seq_spec, seq_spec,
                  pl.BlockSpec((heads_per, 1, A_RAMP), lambda b, p: (p, 0, 0))],
        out_specs=seq_spec,
        out_shape=jax.ShapeDtypeStruct((batch, seq, SELF_WIDTH), BF16),
        scratch_shapes=[pltpu.VMEM((rows, A_BAND), F32),
                        pltpu.VMEM((A_PAD + n_head * A_ROWS, LANES), BF16),
                        pltpu.VMEM((A_PAD + n_head * A_ROWS, LANES), BF16),
                        pltpu.VMEM((rows, A_BAND), F32), pltpu.VMEM((rows, A_BAND), F32),
                        pltpu.VMEM((rows, A_BAND), BF16), pltpu.VMEM((rows, A_BAND), BF16)],
        compiler_params=pltpu.CompilerParams(dimension_semantics=("arbitrary", "arbitrary")),
        name="chunk_attn",
    )(q, k, v, ramp)


def _chunk_bias_ramp(rel_bias):
    heads = rel_bias.shape[0]
    n_far = A_PAD + A_ROWS - REL_CLIP
    n_near = A_RAMP - n_far - (2 * REL_CLIP + 1)
    ramp = jnp.concatenate([jnp.broadcast_to(rel_bias[:, -1:], (heads, n_far)), rel_bias[:, ::-1],
                            jnp.broadcast_to(rel_bias[:, :1], (heads, n_near))], axis=1)
    return ramp.astype(F32).reshape(heads, 1, A_RAMP)


def _diff_attn_kernel(lambda_init, q_ref, k_ref, v_ref, lq1_ref, lk1_ref, lq2_ref, lk2_ref,
                      gain_ref, o_ref, qs_ref, s_a, s_b, p_a, p_b, m_ref, l_ref, acc_ref):
    j = pl.program_id(2)
    rows = 2 * B_ROWS
    q = q_ref[0]
    lane = lax.broadcasted_iota(jnp.int32, q.shape, 1)
    qs_ref[:B_ROWS, :] = jnp.where(lane < HEAD_DIM, q, jnp.zeros_like(q))
    qs_ref[B_ROWS:, :] = jnp.where(lane >= HEAD_DIM, q, jnp.zeros_like(q))
    m_ref[...] = jnp.full(m_ref.shape, NEG_INF, F32)
    l_ref[...] = jnp.zeros(l_ref.shape, F32)
    acc_ref[...] = jnp.zeros(acc_ref.shape, F32)

    def key_rows(t):
        return pl.ds(pl.multiple_of(t * B_ROWS, B_ROWS), B_ROWS)

    def scores(t, s_ref):
        s_ref[...] = _dot_nt(qs_ref[...], k_ref[0, key_rows(t), :])

    def update(t, s_ref, p_ref, diagonal):
        for c in range(rows // B_SUB):
            r = slice(c * B_SUB, (c + 1) * B_SUB)
            s = s_ref[r, :]
            if diagonal:
                base = (c * B_SUB) % B_ROWS
                row_chunk = (lax.broadcasted_iota(jnp.int32, s.shape, 0) + base) // CHUNK
                col_chunk = lax.broadcasted_iota(jnp.int32, s.shape, 1) // CHUNK
                s = jnp.where(col_chunk <= row_chunk, s, NEG_INF)
            m_old = m_ref[r, :]
            m_new = jnp.maximum(m_old, jnp.max(s, axis=-1, keepdims=True))
            alpha = jnp.exp(m_old - m_new)
            e = jnp.exp(s - m_new)
            part = e[:, :LANES]
            for g in range(1, B_ROWS // LANES):
                part = part + e[:, g * LANES:(g + 1) * LANES]
            m_ref[r, :] = m_new
            l_ref[r, :] = alpha * l_ref[r, :] + part
            acc_ref[r, :] = alpha * acc_ref[r, :]
            p_ref[r, :] = e.astype(BF16)
        acc_ref[...] += _dot(p_ref[...], v_ref[0, key_rows(t), :])

    n_full = j
    scores(0, s_a)

    def two_blocks(u, carry):
        t = 2 * u
        scores(t + 1, s_b)
        update(t, s_a, p_a, False)
        scores(t + 2, s_a)
        update(t + 1, s_b, p_b, False)
        return carry

    lax.fori_loop(0, n_full // 2, two_blocks, 0)

    @pl.when(n_full % 2 == 1)
    def _():
        scores(n_full, s_b)
        update(n_full - 1, s_a, p_a, False)
        update(n_full, s_b, p_b, True)

    @pl.when(n_full % 2 == 0)
    def _():
        update(n_full, s_a, p_a, True)

    lam = (jnp.exp(jnp.sum(lq1_ref[...] * lk1_ref[...], axis=-1, keepdims=True))
           - jnp.exp(jnp.sum(lq2_ref[...] * lk2_ref[...], axis=-1, keepdims=True)) + lambda_init)
    inv_l = 1.0 / jnp.sum(l_ref[...], axis=-1, keepdims=True)
    o_all = acc_ref[...] * inv_l
    o = o_all[:B_ROWS] - lam * o_all[B_ROWS:]
    o_ref[0] = (_rms_rows(o, gain_ref[...]) * (1.0 - lambda_init)).astype(BF16)


def _diff_attn_call(q, k, v, lq1, lk1, lq2, lk2, gain, lambda_init):
    batch, seq, _ = q.shape
    rows = 2 * B_ROWS
    q_spec = pl.BlockSpec((1, B_ROWS, LANES), lambda b, h, j: (b, j, h))
    kv_spec = pl.BlockSpec((1, seq, LANES), lambda b, h, j: (b, 0, h))
    vec = lambda w: pl.BlockSpec((1, w), lambda b, h, j: (0, 0))
    return pl.pallas_call(
        functools.partial(_diff_attn_kernel, lambda_init),
        grid=(batch, N_DIFF_HEADS, seq // B_ROWS),
        in_specs=[q_spec, kv_spec, kv_spec, vec(HEAD_DIM), vec(HEAD_DIM), vec(HEAD_DIM),
                  vec(HEAD_DIM), vec(2 * HEAD_DIM)],
        out_specs=q_spec,
        out_shape=jax.ShapeDtypeStruct((batch, seq, SELF_WIDTH), BF16),
        scratch_shapes=[pltpu.VMEM((rows, LANES), BF16),
                        pltpu.VMEM((rows, B_ROWS), F32), pltpu.VMEM((rows, B_ROWS), F32),
                        pltpu.VMEM((rows, B_ROWS), BF16), pltpu.VMEM((rows, B_ROWS), BF16),
                        pltpu.VMEM((rows, 1), F32), pltpu.VMEM((rows, LANES), F32),
                        pltpu.VMEM((rows, LANES), F32)],
        compiler_params=pltpu.CompilerParams(
            dimension_semantics=("arbitrary", "arbitrary", "arbitrary")),
        name="diff_attn",
    )(q, k, v, lq1, lk1, lq2, lk2, gain)


def _rope_tables(positions):
    inv_freq = 1.0 / (ROPE_THETA ** (jnp.arange(0, HEAD_DIM, 2, dtype=F32) / HEAD_DIM))
    ang = positions.astype(F32).reshape(-1, 1) * inv_freq
    cos, sin = jnp.cos(ang), jnp.sin(ang)
    reps = LANES // HEAD_DIM
    return (jnp.tile(jnp.concatenate([cos, cos], axis=1), (1, reps)),
            jnp.tile(jnp.concatenate([-sin, sin], axis=1), (1, reps)))


def kernel(x, mem, positions, ffn1_norm, ffn1_w_gate, ffn1_w_up, ffn1_w_down, mix_norm, mem_norm, w_in, w_mem_kv, mem_q_norm, mem_k_norm, w_out, a_q_norm, a_k_norm, a_rel_bias, b_q_norm, b_k_norm, b_lambda_q1, b_lambda_k1, b_lambda_q2, b_lambda_k2, b_subln, ffn2_norm, ffn2_w_gate, ffn2_w_up, ffn2_w_down):
    batch, seq, _ = x.shape
    tokens = batch * seq
    row = lambda a: a.reshape(1, -1).astype(F32)
    tiled = lambda a, heads: jnp.tile(a.astype(F32), heads).reshape(1, -1)
    bf = lambda a: a.astype(BF16)

    seg = jnp.arange(MXU_COLS) // HEAD_DIM
    seg_ones = (seg[:, None] == seg[None, :]).astype(BF16)
    rope = _rope_tables(positions)

    gk_mem = jnp.stack([tiled(mem_k_norm[i], N_MEM_HEADS) for i in range(DEPTH)])
    mem_k, mem_v = _mem_kv_call(mem, mem_norm.reshape(DEPTH, 1, D_MODEL), bf(w_mem_kv), gk_mem, seg_ones)

    xf = x.reshape(tokens, D_MODEL)
    for i in range(DEPTH):
        j = i // N_MIXERS
        mixer_a = i % N_MIXERS == 0
        xf = _ffn_call(xf, row(ffn1_norm[i]), bf(ffn1_w_gate[i]), bf(ffn1_w_up[i]), bf(ffn1_w_down[i]))
        gq, gk = (a_q_norm[j], a_k_norm[j]) if mixer_a else (b_q_norm[j], b_k_norm[j])
        q, k, v, o_m = _proj_call(i, xf, row(mix_norm[i]), bf(w_in[i]), tiled(gq, N_SELF_HEADS),
                                  tiled(gk, N_SELF_HEADS), tiled(mem_q_norm[i], N_MEM_HEADS),
                                  seg_ones, mem_k, mem_v, None if mixer_a else rope, seq)
        q, k, v = (a.reshape(batch, seq, SELF_WIDTH) for a in (q, k, v))
        if mixer_a:
            o_s = _chunk_attn_call(q, k, v, _chunk_bias_ramp(a_rel_bias[j]))
        else:
            lambda_init = 0.8 - 0.6 * math.exp(-0.3 * i)
            o_s = _diff_attn_call(q, k, v, row(b_lambda_q1[j]), row(b_lambda_k1[j]), row(b_lambda_q2[j]),
                                  row(b_lambda_k2[j]), row(b_subln[j]), lambda_init)
        w_o = bf(w_out[i])
        xf = _ffn_call(xf, row(ffn2_norm[i]), bf(ffn2_w_gate[i]), bf(ffn2_w_up[i]), bf(ffn2_w_down[i]),
                       mix=(o_s.reshape(tokens, SELF_WIDTH), o_m, w_o[:SELF_WIDTH], w_o[SELF_WIDTH:]))
    return xf.reshape(batch, seq, D_MODEL)
```

```python
import functools
import math

import jax
import jax.numpy as jnp
from jax import lax
from jax.experimental import pallas as pl
from jax.experimental.pallas import tpu as pltpu

D_MODEL = 1024
DEPTH = 4
CHUNK = 64
HEAD_DIM = 64
N_SELF_HEADS = 12
N_DIFF_HEADS = 6
N_MEM_HEADS = 4
N_MEM = 256
SELF_WIDTH = N_SELF_HEADS * HEAD_DIM
MEM_WIDTH = N_MEM_HEADS * HEAD_DIM
IN_WIDTH = 3 * SELF_WIDTH + MEM_WIDTH
LEFT_CHUNKS = 8
REL_CLIP = 128
D_FF = 2816
ROPE_THETA = 10000.0
EPS = 1e-6
NEG_INF = -1e30
N_MIXERS = 2

LANES = 128
MXU_COLS = 256
VMEM_LIMIT = 56 * 1024 * 1024

FFN_ROWS = 512
PROJ_ROWS = 512
A_ROWS = 4 * CHUNK
A_PAD = LEFT_CHUNKS * CHUNK
A_BAND = A_ROWS + A_PAD
A_RAMP = A_BAND + A_ROWS
A_SUB = 128
B_ROWS = 512
B_SUB = 64

F32 = jnp.float32
BF16 = jnp.bfloat16


def _dot(a, b):
    return jnp.dot(a, b, preferred_element_type=F32)


def _dot_nt(a, b):
    return lax.dot_general(a, b, (((1,), (1,)), ((), ())), preferred_element_type=F32)


def _rms_rows(x, gain):
    ms = jnp.mean(x * x, axis=-1, keepdims=True)
    return x * lax.rsqrt(ms + EPS) * gain


def _head_rms(x, gain, seg_ones):
    x2 = x * x
    hi = x2.astype(BF16)
    lo = (x2 - hi.astype(F32)).astype(BF16)
    parts = []
    for g in range(x.shape[1] // MXU_COLS):
        sl = slice(g * MXU_COLS, (g + 1) * MXU_COLS)
        parts.append(_dot(hi[:, sl], seg_ones) + _dot(lo[:, sl], seg_ones))
    ss = parts[0] if len(parts) == 1 else jnp.concatenate(parts, axis=1)
    return x * lax.rsqrt(ss * (1.0 / HEAD_DIM) + EPS) * gain


def _rope(x, cos_t, sin_t):
    width = x.shape[1]
    reps = width // LANES
    c = jnp.concatenate([cos_t] * reps, axis=1)
    s = jnp.concatenate([sin_t] * reps, axis=1)
    half = HEAD_DIM // 2
    lane = lax.broadcasted_iota(jnp.int32, x.shape, 1)
    fwd = pltpu.roll(x, width - half, axis=1)
    bwd = pltpu.roll(x, half, axis=1)
    partner = jnp.where((lane % HEAD_DIM) < half, fwd, bwd)
    return x * c + partner * s


def _swiglu_residual(x, gain, wg_ref, wu_ref, wd_ref):
    h = _rms_rows(x, gain).astype(BF16)
    gate = _dot(h, wg_ref[...])
    up = _dot(h, wu_ref[...])
    act = (gate * (1.0 / (1.0 + jnp.exp(-gate))) * up).astype(BF16)
    return x + 0.5 * _dot(act, wd_ref[...])


def _ffn_kernel(x_ref, g_ref, wg_ref, wu_ref, wd_ref, o_ref):
    o_ref[...] = _swiglu_residual(x_ref[...], g_ref[...], wg_ref, wu_ref, wd_ref)


def _out_ffn_kernel(x_ref, os_ref, om_ref, wos_ref, wom_ref, g_ref, wg_ref, wu_ref, wd_ref, o_ref):
    x = x_ref[...] + _dot(os_ref[...], wos_ref[...]) + _dot(om_ref[...], wom_ref[...])
    o_ref[...] = _swiglu_residual(x, g_ref[...], wg_ref, wu_ref, wd_ref)


def _resident(shape):
    return pl.BlockSpec(shape, lambda i: (0,) * len(shape), pipeline_mode=pl.Buffered(1))


def _layer_resident(layer, shape, row_block=0):
    return pl.BlockSpec((None,) + shape, lambda i: (layer, row_block) + (0,) * (len(shape) - 1),
                        pipeline_mode=pl.Buffered(1))


def _ffn_call(layer, x, gain, wg, wu, wd, mix=None):
    tokens = x.shape[0]
    rows = pl.BlockSpec((FFN_ROWS, D_MODEL), lambda i: (i, 0))
    ffn_specs = [_layer_resident(layer, (1, D_MODEL)), _layer_resident(layer, (D_MODEL, D_FF)),
                 _layer_resident(layer, (D_MODEL, D_FF)), _layer_resident(layer, (D_FF, D_MODEL))]
    if mix is None:
        body, ins, specs, name = _ffn_kernel, (x, gain, wg, wu, wd), [rows] + ffn_specs, "ffn"
    else:
        o_s, o_m, w_out = mix
        body, name = _out_ffn_kernel, "out_ffn"
        ins = (x, o_s, o_m, w_out, w_out, gain, wg, wu, wd)
        specs = [rows,
                 pl.BlockSpec((FFN_ROWS, SELF_WIDTH), lambda i: (i, 0)),
                 pl.BlockSpec((FFN_ROWS, MEM_WIDTH), lambda i: (i, 0)),
                 _layer_resident(layer, (SELF_WIDTH, D_MODEL)),
                 _layer_resident(layer, (MEM_WIDTH, D_MODEL), SELF_WIDTH // MEM_WIDTH)] + ffn_specs
    return pl.pallas_call(
        body,
        grid=(tokens // FFN_ROWS,),
        in_specs=specs,
        out_specs=rows,
        out_shape=jax.ShapeDtypeStruct((tokens, D_MODEL), F32),
        compiler_params=pltpu.CompilerParams(dimension_semantics=("arbitrary",),
                                             vmem_limit_bytes=VMEM_LIMIT),
        name=name,
    )(*ins)


def _mem_kv_kernel(mem_ref, g_ref, w_ref, gk_ref, ones_ref, k_ref, v_ref):
    h = _rms_rows(mem_ref[0], g_ref[0]).astype(BF16)
    kv = _dot(h, w_ref[0])
    k_ref[0, 0] = _head_rms(kv[:, :MEM_WIDTH], gk_ref[0], ones_ref[...]).astype(BF16)
    v_ref[0, 0] = kv[:, MEM_WIDTH:].astype(BF16)


def _mem_kv_call(mem, mem_norm, w_mem_kv, gk_tiled, seg_ones):
    batch = mem.shape[0]
    out = jax.ShapeDtypeStruct((DEPTH, batch, N_MEM, MEM_WIDTH), BF16)
    kv_spec = pl.BlockSpec((1, 1, N_MEM, MEM_WIDTH), lambda l, b: (l, b, 0, 0))
    return pl.pallas_call(
        _mem_kv_kernel,
        grid=(DEPTH, batch),
        in_specs=[pl.BlockSpec((1, N_MEM, D_MODEL), lambda l, b: (b, 0, 0)),
                  pl.BlockSpec((1, 1, D_MODEL), lambda l, b: (l, 0, 0)),
                  pl.BlockSpec((1, D_MODEL, 2 * MEM_WIDTH), lambda l, b: (l, 0, 0)),
                  pl.BlockSpec((1, 1, MEM_WIDTH), lambda l, b: (l, 0, 0)),
                  pl.BlockSpec((MXU_COLS, MXU_COLS), lambda l, b: (0, 0))],
        out_specs=[kv_spec, kv_spec],
        out_shape=[out, out],
        compiler_params=pltpu.CompilerParams(dimension_semantics=("arbitrary", "arbitrary")),
        name="mem_kv",
    )(mem, mem_norm, w_mem_kv, gk_tiled, seg_ones)


def _proj_kernel(use_rope, x_ref, g_ref, w_ref, gq_ref, gk_ref, gm_ref, ones_ref, km_ref, vm_ref,
                 *rest):
    if use_rope:
        cos_ref, sin_ref, q_ref, k_ref, v_ref, om_ref = rest
    else:
        q_ref, k_ref, v_ref, om_ref = rest
    seg_ones = ones_ref[...]
    h = _rms_rows(x_ref[...], g_ref[...]).astype(BF16)
    proj = _dot(h, w_ref[...])
    q = _head_rms(proj[:, :SELF_WIDTH], gq_ref[...], seg_ones)
    k = _head_rms(proj[:, SELF_WIDTH:2 * SELF_WIDTH], gk_ref[...], seg_ones)
    if use_rope:
        q = _rope(q, cos_ref[...], sin_ref[...])
        k = _rope(k, cos_ref[...], sin_ref[...])
    scale = HEAD_DIM ** -0.5
    q_ref[...] = (q * scale).astype(BF16)
    k_ref[...] = k.astype(BF16)
    v_ref[...] = proj[:, 2 * SELF_WIDTH:3 * SELF_WIDTH].astype(BF16)

    qm = (_head_rms(proj[:, 3 * SELF_WIDTH:], gm_ref[...], seg_ones) * scale).astype(BF16)
    km = km_ref[0, 0]
    vm = vm_ref[0, 0]
    outs = []
    for hd in range(N_MEM_HEADS):
        sl = slice(hd * HEAD_DIM, (hd + 1) * HEAD_DIM)
        s = _dot_nt(qm[:, sl], km[:, sl])
        e = jnp.exp(s - jnp.max(s, axis=-1, keepdims=True))
        inv = 1.0 / jnp.sum(e, axis=-1, keepdims=True)
        outs.append(_dot(e.astype(BF16), vm[:, sl]) * inv)
    om_ref[...] = jnp.concatenate(outs, axis=1).astype(BF16)


def _proj_call(layer, x, gain, w_in, gq, gk, gm, seg_ones, mem_k, mem_v, rope, seq):
    tokens = x.shape[0]
    per_seq = seq // PROJ_ROWS
    rows = lambda w: pl.BlockSpec((PROJ_ROWS, w), lambda i: (i, 0))
    mem_spec = pl.BlockSpec((1, 1, N_MEM, MEM_WIDTH), lambda i: (layer, i // per_seq, 0, 0))
    specs = [rows(D_MODEL), _layer_resident(layer, (1, D_MODEL)), _layer_resident(layer, (D_MODEL, IN_WIDTH)),
             _resident((1, SELF_WIDTH)), _resident((1, SELF_WIDTH)), _resident((1, MEM_WIDTH)),
             _resident((MXU_COLS, MXU_COLS)), mem_spec, mem_spec]
    ins = [x, gain, w_in, gq, gk, gm, seg_ones, mem_k, mem_v]
    if rope is not None:
        specs += [rows(LANES), rows(LANES)]
        ins += list(rope)
    wide = jax.ShapeDtypeStruct((tokens, SELF_WIDTH), BF16)
    return pl.pallas_call(
        functools.partial(_proj_kernel, rope is not None),
        grid=(tokens // PROJ_ROWS,),
        in_specs=specs,
        out_specs=[rows(SELF_WIDTH), rows(SELF_WIDTH), rows(SELF_WIDTH), rows(MEM_WIDTH)],
        out_shape=[wide, wide, wide, jax.ShapeDtypeStruct((tokens, MEM_WIDTH), BF16)],
        compiler_params=pltpu.CompilerParams(dimension_semantics=("arbitrary",),
                                             vmem_limit_bytes=VMEM_LIMIT),
        name="proj_rope" if rope is not None else "proj",
    )(*ins)


def _chunk_attn_kernel(q_ref, k_ref, v_ref, ramp_ref, o_ref, tab_ref, kpad, vpad, s_a, s_b, p_a, p_b):
    seq = q_ref.shape[1]
    n_blocks = seq // A_ROWS
    n_head = A_PAD // A_ROWS
    rows = 2 * A_ROWS

    row_chunk = lax.broadcasted_iota(jnp.int32, (A_ROWS, A_BAND), 0) // CHUNK
    col_chunk = lax.broadcasted_iota(jnp.int32, (A_ROWS, A_BAND), 1) // CHUNK
    allowed = (col_chunk >= row_chunk) & (col_chunk <= row_chunk + LEFT_CHUNKS)
    for hd in range(LANES // HEAD_DIM):
        ramp = jnp.broadcast_to(ramp_ref[hd], (A_ROWS, A_RAMP))
        toeplitz = pltpu.roll(ramp, A_RAMP - A_ROWS, axis=1, stride=1, stride_axis=0)
        tab_ref[hd * A_ROWS:(hd + 1) * A_ROWS, :] = jnp.where(allowed, toeplitz[:, :A_BAND], NEG_INF)

    kpad[:A_PAD, :] = jnp.zeros((A_PAD, LANES), BF16)
    vpad[:A_PAD, :] = jnp.zeros((A_PAD, LANES), BF16)
    kpad[A_PAD:, :] = k_ref[0, :n_head * A_ROWS, :]
    vpad[A_PAD:, :] = v_ref[0, :n_head * A_ROWS, :]

    def band(ref, pad_ref, t):
        if isinstance(t, int) and t < n_head:
            return pad_ref[t * A_ROWS:t * A_ROWS + A_BAND, :]
        return ref[0, pl.ds(pl.multiple_of((t - n_head) * A_ROWS, A_ROWS), A_BAND), :]

    def block_rows(t):
        return pl.ds(pl.multiple_of(t * A_ROWS, A_ROWS), A_ROWS)

    lane = lax.broadcasted_iota(jnp.int32, (A_ROWS, LANES), 1)

    def scores(t, s_ref):
        q = q_ref[0, block_rows(t), :]
        stacked = jnp.concatenate([jnp.where(lane < HEAD_DIM, q, jnp.zeros_like(q)),
                                   jnp.where(lane >= HEAD_DIM, q, jnp.zeros_like(q))], axis=0)
        s_ref[...] = _dot_nt(stacked, band(k_ref, kpad, t))

    def update(t, s_ref, p_ref, first_slot):
        inv = []
        for c in range(rows // A_SUB):
            r = slice(c * A_SUB, (c + 1) * A_SUB)
            s = s_ref[r, :] + tab_ref[r, :]
            if first_slot:
                slot = lax.broadcasted_iota(jnp.int32, s.shape, 1)
                s = jnp.where(slot >= first_slot, s, NEG_INF)
            e = jnp.exp(s - jnp.max(s, axis=-1, keepdims=True))
            part = e[:, :LANES]
            for g in range(1, A_BAND // LANES):
                part = part + e[:, g * LANES:(g + 1) * LANES]
            inv.append(1.0 / jnp.sum(part, axis=-1, keepdims=True))
            p_ref[r, :] = e.astype(BF16)
        pv = _dot(p_ref[...], band(v_ref, vpad, t)) * jnp.concatenate(inv, axis=0)
        o_ref[0, block_rows(t), :] = jnp.where(lane < HEAD_DIM, pv[:A_ROWS], pv[A_ROWS:]).astype(BF16)

    scores(0, s_a)
    scores(1, s_b)
    update(0, s_a, p_a, A_PAD)
    scores(2, s_a)
    update(1, s_b, p_b, A_PAD - A_ROWS)

    def two_blocks(u, carry):
        t = 2 * u
        scores(t + 1, s_b)
        update(t, s_a, p_a, 0)
        scores(jnp.minimum(t + 2, n_blocks - 1), s_a)
        update(t + 1, s_b, p_b, 0)
        return carry

    lax.fori_loop(1, n_blocks // 2, two_blocks, 0)


def _chunk_attn_call(q, k, v, ramp):
    batch, seq, _ = q.shape
    rows = 2 * A_ROWS
    heads_per = LANES // HEAD_DIM
    n_head = A_PAD // A_ROWS
    seq_spec = pl.BlockSpec((1, seq, LANES), lambda b, p: (b, 0, p))
    return pl.pallas_call(
        _chunk_attn_kernel,
        grid=(batch, SELF_WIDTH // LANES),
        in_specs=[seq_spec, seq_spec, seq_spec,
                  pl.BlockSpec((heads_per, 1, A_RAMP), lambda b, p: (p, 0, 0))],
        out_specs=seq_spec,
        out_shape=jax.ShapeDtypeStruct((batch, seq, SELF_WIDTH), BF16),
        scratch_shapes=[pltpu.VMEM((rows, A_BAND), F32),
                        pltpu.VMEM((A_PAD + n_head * A_ROWS, LANES), BF16),
                        pltpu.VMEM((A_PAD + n_head * A_ROWS, LANES), BF16),
                        pltpu.VMEM((rows, A_BAND), F32), pltpu.VMEM((rows, A_BAND), F32),
                        pltpu.VMEM((rows, A_BAND), BF16), pltpu.VMEM((rows, A_BAND), BF16)],
        compiler_params=pltpu.CompilerParams(dimension_semantics=("arbitrary", "arbitrary")),
        name="chunk_attn",
    )(q, k, v, ramp)


def _chunk_bias_ramp(rel_bias):
    heads = rel_bias.shape[0]
    n_far = A_PAD + A_ROWS - REL_CLIP
    n_near = A_RAMP - n_far - (2 * REL_CLIP + 1)
    ramp = jnp.concatenate([jnp.broadcast_to(rel_bias[:, -1:], (heads, n_far)), rel_bias[:, ::-1],
                            jnp.broadcast_to(rel_bias[:, :1], (heads, n_near))], axis=1)
    return ramp.astype(F32).reshape(heads, 1, A_RAMP)


def _diff_attn_kernel(lambda_init, q_ref, k_ref, v_ref, lq1_ref, lk1_ref, lq2_ref, lk2_ref,
                      gain_ref, o_ref, qs_ref, s_a, s_b, p_a, p_b, m_ref, l_ref, acc_ref):
    j = pl.program_id(2)
    rows = 2 * B_ROWS
    q = q_ref[0]
    lane = lax.broadcasted_iota(jnp.int32, q.shape, 1)
    qs_ref[:B_ROWS, :] = jnp.where(lane < HEAD_DIM, q, jnp.zeros_like(q))
    qs_ref[B_ROWS:, :] = jnp.where(lane >= HEAD_DIM, q, jnp.zeros_like(q))
    m_ref[...] = jnp.full(m_ref.shape, NEG_INF, F32)
    l_ref[...] = jnp.zeros(l_ref.shape, F32)
    acc_ref[...] = jnp.zeros(acc_ref.shape, F32)

    def key_rows(t):
        return pl.ds(pl.multiple_of(t * B_ROWS, B_ROWS), B_ROWS)

    def scores(t, s_ref):
        s_ref[...] = _dot_nt(qs_ref[...], k_ref[0, key_rows(t), :])

    def update(t, s_ref, p_ref, diagonal):
        for c in range(rows // B_SUB):
            r = slice(c * B_SUB, (c + 1) * B_SUB)
            s = s_ref[r, :]
            if diagonal:
                base = (c * B_SUB) % B_ROWS
                row_chunk = (lax.broadcasted_iota(jnp.int32, s.shape, 0) + base) // CHUNK
                col_chunk = lax.broadcasted_iota(jnp.int32, s.shape, 1) // CHUNK
                s = jnp.where(col_chunk <= row_chunk, s, NEG_INF)
            m_old = m_ref[r, :]
            m_new = jnp.maximum(m_old, jnp.max(s, axis=-1, keepdims=True))
            alpha = jnp.exp(m_old - m_new)
            e = jnp.exp(s - m_new)
            part = e[:, :LANES]
            for g in range(1, B_ROWS // LANES):
                part = part + e[:, g * LANES:(g + 1) * LANES]
            m_ref[r, :] = m_new
            l_ref[r, :] = alpha * l_ref[r, :] + part
            acc_ref[r, :] = alpha * acc_ref[r, :]
            p_ref[r, :] = e.astype(BF16)
        acc_ref[...] += _dot(p_ref[...], v_ref[0, key_rows(t), :])

    n_full = j
    scores(0, s_a)

    def two_blocks(u, carry):
        t = 2 * u
        scores(t + 1, s_b)
        update(t, s_a, p_a, False)
        scores(t + 2, s_a)
        update(t + 1, s_b, p_b, False)
        return carry

    lax.fori_loop(0, n_full // 2, two_blocks, 0)

    @pl.when(n_full % 2 == 1)
    def _():
        scores(n_full, s_b)
        update(n_full - 1, s_a, p_a, False)
        update(n_full, s_b, p_b, True)

    @pl.when(n_full % 2 == 0)
    def _():
        update(n_full, s_a, p_a, True)

    lam = (jnp.exp(jnp.sum(lq1_ref[...] * lk1_ref[...], axis=-1, keepdims=True))
           - jnp.exp(jnp.sum(lq2_ref[...] * lk2_ref[...], axis=-1, keepdims=True)) + lambda_init)
    inv_l = 1.0 / jnp.sum(l_ref[...], axis=-1, keepdims=True)
    o_all = acc_ref[...] * inv_l
    o = o_all[:B_ROWS] - lam * o_all[B_ROWS:]
    o_ref[0] = (_rms_rows(o, gain_ref[...]) * (1.0 - lambda_init)).astype(BF16)


def _diff_attn_call(q, k, v, lq1, lk1, lq2, lk2, gain, lambda_init):
    batch, seq, _ = q.shape
    rows = 2 * B_ROWS
    q_spec = pl.BlockSpec((1, B_ROWS, LANES), lambda b, h, j: (b, j, h))
    kv_spec = pl.BlockSpec((1, seq, LANES), lambda b, h, j: (b, 0, h))
    vec = lambda w: pl.BlockSpec((1, w), lambda b, h, j: (0, 0))
    return pl.pallas_call(
        functools.partial(_diff_attn_kernel, lambda_init),
        grid=(batch, N_DIFF_HEADS, seq // B_ROWS),
        in_specs=[q_spec, kv_spec, kv_spec, vec(HEAD_DIM), vec(HEAD_DIM), vec(HEAD_DIM),
                  vec(HEAD_DIM), vec(2 * HEAD_DIM)],
        out_specs=q_spec,
        out_shape=jax.ShapeDtypeStruct((batch, seq, SELF_WIDTH), BF16),
        scratch_shapes=[pltpu.VMEM((rows, LANES), BF16),
                        pltpu.VMEM((rows, B_ROWS), F32), pltpu.VMEM((rows, B_ROWS), F32),
                        pltpu.VMEM((rows, B_ROWS), BF16), pltpu.VMEM((rows, B_ROWS), BF16),
                        pltpu.VMEM((rows, 1), F32), pltpu.VMEM((rows, LANES), F32),
                        pltpu.VMEM((rows, LANES), F32)],
        compiler_params=pltpu.CompilerParams(
            dimension_semantics=("arbitrary", "arbitrary", "arbitrary")),
        name="diff_attn",
    )(q, k, v, lq1, lk1, lq2, lk2, gain)


def _rope_tables(positions):
    inv_freq = 1.0 / (ROPE_THETA ** (jnp.arange(0, HEAD_DIM, 2, dtype=F32) / HEAD_DIM))
    ang = positions.astype(F32).reshape(-1, 1) * inv_freq
    cos, sin = jnp.cos(ang), jnp.sin(ang)
    reps = LANES // HEAD_DIM
    return (jnp.tile(jnp.concatenate([cos, cos], axis=1), (1, reps)),
            jnp.tile(jnp.concatenate([-sin, sin], axis=1), (1, reps)))


def kernel(x, mem, positions, ffn1_norm, ffn1_w_gate, ffn1_w_up, ffn1_w_down, mix_norm, mem_norm, w_in, w_mem_kv, mem_q_norm, mem_k_norm, w_out, a_q_norm, a_k_norm, a_rel_bias, b_q_norm, b_k_norm, b_lambda_q1, b_lambda_k1, b_lambda_q2, b_lambda_k2, b_subln, ffn2_norm, ffn2_w_gate, ffn2_w_up, ffn2_w_down):
    batch, seq, _ = x.shape
    tokens = batch * seq
    row = lambda a: a.reshape(1, -1).astype(F32)
    tiled = lambda a, heads: jnp.tile(a.astype(F32), heads).reshape(1, -1)
    bf = lambda a: a.astype(BF16)

    seg = jnp.arange(MXU_COLS) // HEAD_DIM
    seg_ones = (seg[:, None] == seg[None, :]).astype(BF16)
    rope = _rope_tables(positions)

    gk_mem = jnp.stack([tiled(mem_k_norm[i], N_MEM_HEADS) for i in range(DEPTH)])
    mem_k, mem_v = _mem_kv_call(mem, mem_norm.reshape(DEPTH, 1, D_MODEL), bf(w_mem_kv), gk_mem, seg_ones)

    stacked_gain = lambda a: a.reshape(DEPTH, 1, D_MODEL).astype(F32)
    ffn1 = (stacked_gain(ffn1_norm), bf(ffn1_w_gate), bf(ffn1_w_up), bf(ffn1_w_down))
    ffn2 = (stacked_gain(ffn2_norm), bf(ffn2_w_gate), bf(ffn2_w_up), bf(ffn2_w_down))
    mix_gain, w_in_bf, w_out_bf = stacked_gain(mix_norm), bf(w_in), bf(w_out)

    xf = x.reshape(tokens, D_MODEL)
    for i in range(DEPTH):
        j = i // N_MIXERS
        mixer_a = i % N_MIXERS == 0
        xf = _ffn_call(i, xf, *ffn1)
        gq, gk = (a_q_norm[j], a_k_norm[j]) if mixer_a else (b_q_norm[j], b_k_norm[j])
        q, k, v, o_m = _proj_call(i, xf, mix_gain, w_in_bf, tiled(gq, N_SELF_HEADS),
                                  tiled(gk, N_SELF_HEADS), tiled(mem_q_norm[i], N_MEM_HEADS),
                                  seg_ones, mem_k, mem_v, None if mixer_a else rope, seq)
        q, k, v = (a.reshape(batch, seq, SELF_WIDTH) for a in (q, k, v))
        if mixer_a:
            o_s = _chunk_attn_call(q, k, v, _chunk_bias_ramp(a_rel_bias[j]))
        else:
            lambda_init = 0.8 - 0.6 * math.exp(-0.3 * i)
            o_s = _diff_attn_call(q, k, v, row(b_lambda_q1[j]), row(b_lambda_k1[j]), row(b_lambda_q2[j]),
                                  row(b_lambda_k2[j]), row(b_subln[j]), lambda_init)
        xf = _ffn_call(i, xf, *ffn2, mix=(o_s.reshape(tokens, SELF_WIDTH), o_m, w_out_bf))
    return xf.reshape(batch, seq, D_MODEL)
```

```python
import functools
import math

import jax
import jax.numpy as jnp
from jax import lax
from jax.experimental import pallas as pl
from jax.experimental.pallas import tpu as pltpu

D_MODEL = 1024
DEPTH = 4
CHUNK = 64
HEAD_DIM = 64
N_SELF_HEADS = 12
N_DIFF_HEADS = 6
N_MEM_HEADS = 4
N_MEM = 256
SELF_WIDTH = N_SELF_HEADS * HEAD_DIM
MEM_WIDTH = N_MEM_HEADS * HEAD_DIM
IN_WIDTH = 3 * SELF_WIDTH + MEM_WIDTH
LEFT_CHUNKS = 8
REL_CLIP = 128
D_FF = 2816
ROPE_THETA = 10000.0
EPS = 1e-6
NEG_INF = -1e30
N_MIXERS = 2

LANES = 128
MXU_COLS = 256
VMEM_LIMIT = 56 * 1024 * 1024

FFN_ROWS = 512
PROJ_ROWS = 512
A_ROWS = 4 * CHUNK
A_PAD = LEFT_CHUNKS * CHUNK
A_BAND = A_ROWS + A_PAD
A_RAMP = A_BAND + A_ROWS
A_SUB = 128
B_ROWS = 512
B_SUB = 64
B_GROUP = 2

F32 = jnp.float32
BF16 = jnp.bfloat16


def _dot(a, b):
    return jnp.dot(a, b, preferred_element_type=F32)


def _dot_nt(a, b):
    return lax.dot_general(a, b, (((1,), (1,)), ((), ())), preferred_element_type=F32)


def _rms_rows(x, gain):
    ms = jnp.mean(x * x, axis=-1, keepdims=True)
    return x * lax.rsqrt(ms + EPS) * gain


def _head_rms(x, gain, seg_ones):
    x2 = x * x
    hi = x2.astype(BF16)
    lo = (x2 - hi.astype(F32)).astype(BF16)
    parts = []
    for g in range(x.shape[1] // MXU_COLS):
        sl = slice(g * MXU_COLS, (g + 1) * MXU_COLS)
        parts.append(_dot(hi[:, sl], seg_ones) + _dot(lo[:, sl], seg_ones))
    ss = parts[0] if len(parts) == 1 else jnp.concatenate(parts, axis=1)
    return x * lax.rsqrt(ss * (1.0 / HEAD_DIM) + EPS) * gain


def _rope(x, cos_t, sin_t):
    width = x.shape[1]
    reps = width // LANES
    c = jnp.concatenate([cos_t] * reps, axis=1)
    s = jnp.concatenate([sin_t] * reps, axis=1)
    half = HEAD_DIM // 2
    lane = lax.broadcasted_iota(jnp.int32, x.shape, 1)
    fwd = pltpu.roll(x, width - half, axis=1)
    bwd = pltpu.roll(x, half, axis=1)
    partner = jnp.where((lane % HEAD_DIM) < half, fwd, bwd)
    return x * c + partner * s


def _swiglu_residual(x, gain, wg_ref, wu_ref, wd_ref):
    h = _rms_rows(x, gain).astype(BF16)
    gate = _dot(h, wg_ref[...])
    up = _dot(h, wu_ref[...])
    act = (gate * (1.0 / (1.0 + jnp.exp(-gate))) * up).astype(BF16)
    return x + 0.5 * _dot(act, wd_ref[...])


def _ffn_kernel(x_ref, g_ref, wg_ref, wu_ref, wd_ref, o_ref):
    o_ref[...] = _swiglu_residual(x_ref[...], g_ref[...], wg_ref, wu_ref, wd_ref)


def _out_ffn_kernel(x_ref, os_ref, om_ref, wos_ref, wom_ref, g_ref, wg_ref, wu_ref, wd_ref, o_ref):
    x = x_ref[...] + _dot(os_ref[...], wos_ref[...]) + _dot(om_ref[...], wom_ref[...])
    o_ref[...] = _swiglu_residual(x, g_ref[...], wg_ref, wu_ref, wd_ref)


def _resident(shape):
    return pl.BlockSpec(shape, lambda i: (0,) * len(shape), pipeline_mode=pl.Buffered(1))


def _layer_resident(layer, shape, row_block=0):
    return pl.BlockSpec((None,) + shape, lambda i: (layer, row_block) + (0,) * (len(shape) - 1),
                        pipeline_mode=pl.Buffered(1))


def _ffn_call(layer, x, gain, wg, wu, wd, mix=None):
    tokens = x.shape[0]
    rows = pl.BlockSpec((FFN_ROWS, D_MODEL), lambda i: (i, 0))
    ffn_specs = [_layer_resident(layer, (1, D_MODEL)), _layer_resident(layer, (D_MODEL, D_FF)),
                 _layer_resident(layer, (D_MODEL, D_FF)), _layer_resident(layer, (D_FF, D_MODEL))]
    if mix is None:
        body, ins, specs, name = _ffn_kernel, (x, gain, wg, wu, wd), [rows] + ffn_specs, "ffn"
    else:
        o_s, o_m, w_out = mix
        body, name = _out_ffn_kernel, "out_ffn"
        ins = (x, o_s, o_m, w_out, w_out, gain, wg, wu, wd)
        specs = [rows,
                 pl.BlockSpec((FFN_ROWS, SELF_WIDTH), lambda i: (i, 0)),
                 pl.BlockSpec((FFN_ROWS, MEM_WIDTH), lambda i: (i, 0)),
                 _layer_resident(layer, (SELF_WIDTH, D_MODEL)),
                 _layer_resident(layer, (MEM_WIDTH, D_MODEL), SELF_WIDTH // MEM_WIDTH)] + ffn_specs
    return pl.pallas_call(
        body,
        grid=(tokens // FFN_ROWS,),
        in_specs=specs,
        out_specs=rows,
        out_shape=jax.ShapeDtypeStruct((tokens, D_MODEL), F32),
        compiler_params=pltpu.CompilerParams(dimension_semantics=("arbitrary",),
                                             vmem_limit_bytes=VMEM_LIMIT),
        name=name,
    )(*ins)


def _mem_kv_kernel(mem_ref, g_ref, w_ref, gk_ref, ones_ref, k_ref, v_ref):
    h = _rms_rows(mem_ref[0], g_ref[0]).astype(BF16)
    kv = _dot(h, w_ref[0])
    k_ref[0, 0] = _head_rms(kv[:, :MEM_WIDTH], gk_ref[0], ones_ref[...]).astype(BF16)
    v_ref[0, 0] = kv[:, MEM_WIDTH:].astype(BF16)


def _mem_kv_call(mem, mem_norm, w_mem_kv, gk_tiled, seg_ones):
    batch = mem.shape[0]
    out = jax.ShapeDtypeStruct((DEPTH, batch, N_MEM, MEM_WIDTH), BF16)
    kv_spec = pl.BlockSpec((1, 1, N_MEM, MEM_WIDTH), lambda l, b: (l, b, 0, 0))
    return pl.pallas_call(
        _mem_kv_kernel,
        grid=(DEPTH, batch),
        in_specs=[pl.BlockSpec((1, N_MEM, D_MODEL), lambda l, b: (b, 0, 0)),
                  pl.BlockSpec((1, 1, D_MODEL), lambda l, b: (l, 0, 0)),
                  pl.BlockSpec((1, D_MODEL, 2 * MEM_WIDTH), lambda l, b: (l, 0, 0)),
                  pl.BlockSpec((1, 1, MEM_WIDTH), lambda l, b: (l, 0, 0)),
                  pl.BlockSpec((MXU_COLS, MXU_COLS), lambda l, b: (0, 0))],
        out_specs=[kv_spec, kv_spec],
        out_shape=[out, out],
        compiler_params=pltpu.CompilerParams(dimension_semantics=("arbitrary", "arbitrary")),
        name="mem_kv",
    )(mem, mem_norm, w_mem_kv, gk_tiled, seg_ones)


def _proj_kernel(use_rope, x_ref, g_ref, w_ref, gq_ref, gk_ref, gm_ref, ones_ref, km_ref, vm_ref,
                 *rest):
    if use_rope:
        cos_ref, sin_ref, q_ref, k_ref, v_ref, om_ref = rest
    else:
        q_ref, k_ref, v_ref, om_ref = rest
    seg_ones = ones_ref[...]
    h = _rms_rows(x_ref[...], g_ref[...]).astype(BF16)

    def section(first, width):
        return _dot(h, w_ref[:, first:first + width])

    scale = HEAD_DIM ** -0.5
    q = _head_rms(section(0, SELF_WIDTH), gq_ref[...], seg_ones)
    if use_rope:
        q = _rope(q, cos_ref[...], sin_ref[...])
    q_ref[...] = (q * scale).astype(BF16)
    k = _head_rms(section(SELF_WIDTH, SELF_WIDTH), gk_ref[...], seg_ones)
    if use_rope:
        k = _rope(k, cos_ref[...], sin_ref[...])
    k_ref[...] = k.astype(BF16)
    v_ref[...] = section(2 * SELF_WIDTH, SELF_WIDTH).astype(BF16)

    qm = (_head_rms(section(3 * SELF_WIDTH, MEM_WIDTH), gm_ref[...], seg_ones) * scale).astype(BF16)
    km = km_ref[0, 0]
    vm = vm_ref[0, 0]
    head_of_lane = lax.broadcasted_iota(jnp.int32, qm.shape, 1) // HEAD_DIM
    om = jnp.zeros(qm.shape, F32)
    for hd in range(N_MEM_HEADS):
        mine = head_of_lane == hd
        s = _dot_nt(jnp.where(mine, qm, jnp.zeros_like(qm)), km)
        e = jnp.exp(s - jnp.max(s, axis=-1, keepdims=True))
        inv = 1.0 / jnp.sum(e, axis=-1, keepdims=True)
        om = jnp.where(mine, _dot(e.astype(BF16), vm) * inv, om)
    om_ref[...] = om.astype(BF16)


def _proj_call(layer, x, gain, w_in, gq, gk, gm, seg_ones, mem_k, mem_v, rope, seq):
    tokens = x.shape[0]
    per_seq = seq // PROJ_ROWS
    rows = lambda w: pl.BlockSpec((PROJ_ROWS, w), lambda i: (i, 0))
    mem_spec = pl.BlockSpec((1, 1, N_MEM, MEM_WIDTH), lambda i: (layer, i // per_seq, 0, 0))
    specs = [rows(D_MODEL), _layer_resident(layer, (1, D_MODEL)), _layer_resident(layer, (D_MODEL, IN_WIDTH)),
             _resident((1, SELF_WIDTH)), _resident((1, SELF_WIDTH)), _resident((1, MEM_WIDTH)),
             _resident((MXU_COLS, MXU_COLS)), mem_spec, mem_spec]
    ins = [x, gain, w_in, gq, gk, gm, seg_ones, mem_k, mem_v]
    if rope is not None:
        specs += [rows(LANES), rows(LANES)]
        ins += list(rope)
    wide = jax.ShapeDtypeStruct((tokens, SELF_WIDTH), BF16)
    return pl.pallas_call(
        functools.partial(_proj_kernel, rope is not None),
        grid=(tokens // PROJ_ROWS,),
        in_specs=specs,
        out_specs=[rows(SELF_WIDTH), rows(SELF_WIDTH), rows(SELF_WIDTH), rows(MEM_WIDTH)],
        out_shape=[wide, wide, wide, jax.ShapeDtypeStruct((tokens, MEM_WIDTH), BF16)],
        compiler_params=pltpu.CompilerParams(dimension_semantics=("arbitrary",),
                                             vmem_limit_bytes=VMEM_LIMIT),
        name="proj_rope" if rope is not None else "proj",
    )(*ins)


def _chunk_attn_kernel(q_ref, k_ref, v_ref, ramp_ref, o_ref, tab_ref, kpad, vpad, s_a, s_b, p_a, p_b):
    seq = q_ref.shape[1]
    n_blocks = seq // A_ROWS
    n_head = A_PAD // A_ROWS
    rows = 2 * A_ROWS

    row_chunk = lax.broadcasted_iota(jnp.int32, (A_ROWS, A_BAND), 0) // CHUNK
    col_chunk = lax.broadcasted_iota(jnp.int32, (A_ROWS, A_BAND), 1) // CHUNK
    allowed = (col_chunk >= row_chunk) & (col_chunk <= row_chunk + LEFT_CHUNKS)
    for hd in range(LANES // HEAD_DIM):
        ramp = jnp.broadcast_to(ramp_ref[hd], (A_ROWS, A_RAMP))
        toeplitz = pltpu.roll(ramp, A_RAMP - A_ROWS, axis=1, stride=1, stride_axis=0)
        tab_ref[hd * A_ROWS:(hd + 1) * A_ROWS, :] = jnp.where(allowed, toeplitz[:, :A_BAND], NEG_INF)

    kpad[:A_PAD, :] = jnp.zeros((A_PAD, LANES), BF16)
    vpad[:A_PAD, :] = jnp.zeros((A_PAD, LANES), BF16)
    kpad[A_PAD:, :] = k_ref[0, :n_head * A_ROWS, :]
    vpad[A_PAD:, :] = v_ref[0, :n_head * A_ROWS, :]

    def band(ref, pad_ref, t):
        if isinstance(t, int) and t < n_head:
            return pad_ref[t * A_ROWS:t * A_ROWS + A_BAND, :]
        return ref[0, pl.ds(pl.multiple_of((t - n_head) * A_ROWS, A_ROWS), A_BAND), :]

    def block_rows(t):
        return pl.ds(pl.multiple_of(t * A_ROWS, A_ROWS), A_ROWS)

    lane = lax.broadcasted_iota(jnp.int32, (A_ROWS, LANES), 1)

    def scores(t, s_ref):
        q = q_ref[0, block_rows(t), :]
        stacked = jnp.concatenate([jnp.where(lane < HEAD_DIM, q, jnp.zeros_like(q)),
                                   jnp.where(lane >= HEAD_DIM, q, jnp.zeros_like(q))], axis=0)
        s_ref[...] = _dot_nt(stacked, band(k_ref, kpad, t))

    def update(t, s_ref, p_ref, first_slot):
        inv = []
        for c in range(rows // A_SUB):
            r = slice(c * A_SUB, (c + 1) * A_SUB)
            s = s_ref[r, :] + tab_ref[r, :]
            if first_slot:
                slot = lax.broadcasted_iota(jnp.int32, s.shape, 1)
                s = jnp.where(slot >= first_slot, s, NEG_INF)
            e = jnp.exp(s - jnp.max(s, axis=-1, keepdims=True))
            part = e[:, :LANES]
            for g in range(1, A_BAND // LANES):
                part = part + e[:, g * LANES:(g + 1) * LANES]
            inv.append(1.0 / jnp.sum(part, axis=-1, keepdims=True))
            p_ref[r, :] = e.astype(BF16)
        pv = _dot(p_ref[...], band(v_ref, vpad, t)) * jnp.concatenate(inv, axis=0)
        o_ref[0, block_rows(t), :] = jnp.where(lane < HEAD_DIM, pv[:A_ROWS], pv[A_ROWS:]).astype(BF16)

    scores(0, s_a)
    scores(1, s_b)
    update(0, s_a, p_a, A_PAD)
    scores(2, s_a)
    update(1, s_b, p_b, A_PAD - A_ROWS)

    def two_blocks(u, carry):
        t = 2 * u
        scores(t + 1, s_b)
        update(t, s_a, p_a, 0)
        scores(jnp.minimum(t + 2, n_blocks - 1), s_a)
        update(t + 1, s_b, p_b, 0)
        return carry

    lax.fori_loop(1, n_blocks // 2, two_blocks, 0)


def _chunk_attn_call(q, k, v, ramp):
    batch, seq, _ = q.shape
    rows = 2 * A_ROWS
    heads_per = LANES // HEAD_DIM
    n_head = A_PAD // A_ROWS
    seq_spec = pl.BlockSpec((1, seq, LANES), lambda b, p: (b, 0, p))
    return pl.pallas_call(
        _chunk_attn_kernel,
        grid=(batch, SELF_WIDTH // LANES),
        in_specs=[seq_spec, seq_spec, seq_spec,
                  pl.BlockSpec((heads_per, 1, A_RAMP), lambda b, p: (p, 0, 0))],
        out_specs=seq_spec,
        out_shape=jax.ShapeDtypeStruct((batch, seq, SELF_WIDTH), BF16),
        scratch_shapes=[pltpu.VMEM((rows, A_BAND), F32),
                        pltpu.VMEM((A_PAD + n_head * A_ROWS, LANES), BF16),
                        pltpu.VMEM((A_PAD + n_head * A_ROWS, LANES), BF16),
                        pltpu.VMEM((rows, A_BAND), F32), pltpu.VMEM((rows, A_BAND), F32),
                        pltpu.VMEM((rows, A_BAND), BF16), pltpu.VMEM((rows, A_BAND), BF16)],
        compiler_params=pltpu.CompilerParams(dimension_semantics=("arbitrary", "arbitrary")),
        name="chunk_attn",
    )(q, k, v, ramp)


def _chunk_bias_ramp(rel_bias):
    heads = rel_bias.shape[0]
    n_far = A_PAD + A_ROWS - REL_CLIP
    n_near = A_RAMP - n_far - (2 * REL_CLIP + 1)
    ramp = jnp.concatenate([jnp.broadcast_to(rel_bias[:, -1:], (heads, n_far)), rel_bias[:, ::-1],
                            jnp.broadcast_to(rel_bias[:, :1], (heads, n_near))], axis=1)
    return ramp.astype(F32).reshape(heads, 1, A_RAMP)


def _diff_attn_kernel(lambda_init, q_ref, k_ref, v_ref, lq1_ref, lk1_ref, lq2_ref, lk2_ref,
                      gain_ref, o_ref, qs_ref, s_a, s_b, p_a, p_b, m_ref, l_ref, acc_ref):
    j = pl.program_id(2)
    rows = 2 * B_ROWS
    q = q_ref[0]
    lane = lax.broadcasted_iota(jnp.int32, q.shape, 1)
    qs_ref[:B_ROWS, :] = jnp.where(lane < HEAD_DIM, q, jnp.zeros_like(q))
    qs_ref[B_ROWS:, :] = jnp.where(lane >= HEAD_DIM, q, jnp.zeros_like(q))
    m_ref[...] = jnp.full(m_ref.shape, NEG_INF, F32)
    l_ref[...] = jnp.zeros(l_ref.shape, F32)
    acc_ref[...] = jnp.zeros(acc_ref.shape, F32)

    def key_rows(t):
        return pl.ds(pl.multiple_of(t * B_ROWS, B_ROWS), B_ROWS)

    def scores(t, s_ref):
        s_ref[...] = _dot_nt(qs_ref[...], k_ref[0, key_rows(t), :])

    def update(t, s_ref, p_ref, diagonal):
        for c in range(rows // B_SUB):
            r = slice(c * B_SUB, (c + 1) * B_SUB)
            s = s_ref[r, :]
            if diagonal:
                base = (c * B_SUB) % B_ROWS
                row_chunk = (lax.broadcasted_iota(jnp.int32, s.shape, 0) + base) // CHUNK
                col_chunk = lax.broadcasted_iota(jnp.int32, s.shape, 1) // CHUNK
                s = jnp.where(col_chunk <= row_chunk, s, NEG_INF)
            m_old = m_ref[r, :]
            m_new = jnp.maximum(m_old, jnp.max(s, axis=-1, keepdims=True))
            alpha = jnp.exp(m_old - m_new)
            e = jnp.exp(s - m_new)
            part = e[:, :LANES]
            for g in range(1, B_ROWS // LANES):
                part = part + e[:, g * LANES:(g + 1) * LANES]
            m_ref[r, :] = m_new
            l_ref[r, :] = alpha * l_ref[r, :] + part
            acc_ref[r, :] = alpha * acc_ref[r, :]
            p_ref[r, :] = e.astype(BF16)
        acc_ref[...] += _dot(p_ref[...], v_ref[0, key_rows(t), :])

    n_full = j
    bufs = ((s_a, p_a), (s_b, p_b))
    scores(0, s_a)

    def full_blocks(first, count):
        for i in range(count):
            scores(first + i + 1, bufs[(i + 1) % 2][0])
            update(first + i, *bufs[i % 2], False)

    def block_group(u, carry):
        full_blocks(B_GROUP * u, B_GROUP)
        return carry

    lax.fori_loop(0, n_full // B_GROUP, block_group, 0)

    left = n_full % B_GROUP
    for count in range(B_GROUP):
        @pl.when(left == count)
        def _(count=count):
            full_blocks(n_full - count, count)
            update(n_full, *bufs[count % 2], True)

    lam = (jnp.exp(jnp.sum(lq1_ref[...] * lk1_ref[...], axis=-1, keepdims=True))
           - jnp.exp(jnp.sum(lq2_ref[...] * lk2_ref[...], axis=-1, keepdims=True)) + lambda_init)
    inv_l = 1.0 / jnp.sum(l_ref[...], axis=-1, keepdims=True)
    o_all = acc_ref[...] * inv_l
    o = o_all[:B_ROWS] - lam * o_all[B_ROWS:]
    o_ref[0] = (_rms_rows(o, gain_ref[...]) * (1.0 - lambda_init)).astype(BF16)


def _diff_attn_call(q, k, v, lq1, lk1, lq2, lk2, gain, lambda_init):
    batch, seq, _ = q.shape
    rows = 2 * B_ROWS
    q_spec = pl.BlockSpec((1, B_ROWS, LANES), lambda b, h, j: (b, j, h))
    kv_spec = pl.BlockSpec((1, seq, LANES), lambda b, h, j: (b, 0, h))
    vec = lambda w: pl.BlockSpec((1, w), lambda b, h, j: (0, 0))
    return pl.pallas_call(
        functools.partial(_diff_attn_kernel, lambda_init),
        grid=(batch, N_DIFF_HEADS, seq // B_ROWS),
        in_specs=[q_spec, kv_spec, kv_spec, vec(HEAD_DIM), vec(HEAD_DIM), vec(HEAD_DIM),
                  vec(HEAD_DIM), vec(2 * HEAD_DIM)],
        out_specs=q_spec,
        out_shape=jax.ShapeDtypeStruct((batch, seq, SELF_WIDTH), BF16),
        scratch_shapes=[pltpu.VMEM((rows, LANES), BF16),
                        pltpu.VMEM((rows, B_ROWS), F32), pltpu.VMEM((rows, B_ROWS), F32),
                        pltpu.VMEM((rows, B_ROWS), BF16), pltpu.VMEM((rows, B_ROWS), BF16),
                        pltpu.VMEM((rows, 1), F32), pltpu.VMEM((rows, LANES), F32),
                        pltpu.VMEM((rows, LANES), F32)],
        compiler_params=pltpu.CompilerParams(
            dimension_semantics=("arbitrary", "arbitrary", "arbitrary")),
        name="diff_attn",
    )(q, k, v, lq1, lk1, lq2, lk2, gain)


def _rope_tables(positions):
    inv_freq = 1.0 / (ROPE_THETA ** (jnp.arange(0, HEAD_DIM, 2, dtype=F32) / HEAD_DIM))
    ang = positions.astype(F32).reshape(-1, 1) * inv_freq
    cos, sin = jnp.cos(ang), jnp.sin(ang)
    reps = LANES // HEAD_DIM
    return (jnp.tile(jnp.concatenate([cos, cos], axis=1), (1, reps)),
            jnp.tile(jnp.concatenate([-sin, sin], axis=1), (1, reps)))


def kernel(x, mem, positions, ffn1_norm, ffn1_w_gate, ffn1_w_up, ffn1_w_down, mix_norm, mem_norm, w_in, w_mem_kv, mem_q_norm, mem_k_norm, w_out, a_q_norm, a_k_norm, a_rel_bias, b_q_norm, b_k_norm, b_lambda_q1, b_lambda_k1, b_lambda_q2, b_lambda_k2, b_subln, ffn2_norm, ffn2_w_gate, ffn2_w_up, ffn2_w_down):
    batch, seq, _ = x.shape
    tokens = batch * seq
    row = lambda a: a.reshape(1, -1).astype(F32)
    tiled = lambda a, heads: jnp.tile(a.astype(F32), heads).reshape(1, -1)
    bf = lambda a: a.astype(BF16)

    seg = jnp.arange(MXU_COLS) // HEAD_DIM
    seg_ones = (seg[:, None] == seg[None, :]).astype(BF16)
    rope = _rope_tables(positions)

    gk_mem = jnp.stack([tiled(mem_k_norm[i], N_MEM_HEADS) for i in range(DEPTH)])
    mem_k, mem_v = _mem_kv_call(mem, mem_norm.reshape(DEPTH, 1, D_MODEL), bf(w_mem_kv), gk_mem, seg_ones)

    stacked_gain = lambda a: a.reshape(DEPTH, 1, D_MODEL).astype(F32)
    ffn1 = (stacked_gain(ffn1_norm), bf(ffn1_w_gate), bf(ffn1_w_up), bf(ffn1_w_down))
    ffn2 = (stacked_gain(ffn2_norm), bf(ffn2_w_gate), bf(ffn2_w_up), bf(ffn2_w_down))
    mix_gain, w_in_bf, w_out_bf = stacked_gain(mix_norm), bf(w_in), bf(w_out)

    xf = x.reshape(tokens, D_MODEL)
    for i in range(DEPTH):
        j = i // N_MIXERS
        mixer_a = i % N_MIXERS == 0
        xf = _ffn_call(i, xf, *ffn1)
        gq, gk = (a_q_norm[j], a_k_norm[j]) if mixer_a else (b_q_norm[j], b_k_norm[j])
        q, k, v, o_m = _proj_call(i, xf, mix_gain, w_in_bf, tiled(gq, N_SELF_HEADS),
                                  tiled(gk, N_SELF_HEADS), tiled(mem_q_norm[i], N_MEM_HEADS),
                                  seg_ones, mem_k, mem_v, None if mixer_a else rope, seq)
        q, k, v = (a.reshape(batch, seq, SELF_WIDTH) for a in (q, k, v))
        if mixer_a:
            o_s = _chunk_attn_call(q, k, v, _chunk_bias_ramp(a_rel_bias[j]))
        else:
            lambda_init = 0.8 - 0.6 * math.exp(-0.3 * i)
            o_s = _diff_attn_call(q, k, v, row(b_lambda_q1[j]), row(b_lambda_k1[j]), row(b_lambda_q2[j]),
                                  row(b_lambda_k2[j]), row(b_subln[j]), lambda_init)
        xf = _ffn_call(i, xf, *ffn2, mix=(o_s.reshape(tokens, SELF_WIDTH), o_m, w_out_bf))
    return xf.reshape(batch, seq, D_MODEL)
```

```python
import functools
import math

import jax
import jax.numpy as jnp
from jax import lax
from jax.experimental import pallas as pl
from jax.experimental.pallas import tpu as pltpu

D_MODEL = 1024
DEPTH = 4
CHUNK = 64
HEAD_DIM = 64
N_SELF_HEADS = 12
N_DIFF_HEADS = 6
N_MEM_HEADS = 4
N_MEM = 256
SELF_WIDTH = N_SELF_HEADS * HEAD_DIM
MEM_WIDTH = N_MEM_HEADS * HEAD_DIM
IN_WIDTH = 3 * SELF_WIDTH + MEM_WIDTH
LEFT_CHUNKS = 8
REL_CLIP = 128
D_FF = 2816
ROPE_THETA = 10000.0
EPS = 1e-6
NEG_INF = -1e30
N_MIXERS = 2

LANES = 128
MXU_COLS = 256
VMEM_LIMIT = 56 * 1024 * 1024

FFN_ROWS = 512
PROJ_ROWS = 512
A_ROWS = 4 * CHUNK
A_PAD = LEFT_CHUNKS * CHUNK
A_BAND = A_ROWS + A_PAD
A_RAMP = A_BAND + A_ROWS
A_SUB = 128
B_ROWS = 512
B_SUB = 64
B_GROUP = 2

F32 = jnp.float32
BF16 = jnp.bfloat16


def _dot(a, b):
    return jnp.dot(a, b, preferred_element_type=F32)


def _dot_nt(a, b):
    return lax.dot_general(a, b, (((1,), (1,)), ((), ())), preferred_element_type=F32)


def _rms_rows(x, gain):
    ms = jnp.mean(x * x, axis=-1, keepdims=True)
    return x * lax.rsqrt(ms + EPS) * gain


def _head_rms(x, gain, seg_ones):
    x2 = x * x
    hi = x2.astype(BF16)
    lo = (x2 - hi.astype(F32)).astype(BF16)
    parts = []
    for g in range(x.shape[1] // MXU_COLS):
        sl = slice(g * MXU_COLS, (g + 1) * MXU_COLS)
        parts.append(_dot(hi[:, sl], seg_ones) + _dot(lo[:, sl], seg_ones))
    ss = parts[0] if len(parts) == 1 else jnp.concatenate(parts, axis=1)
    return x * lax.rsqrt(ss * (1.0 / HEAD_DIM) + EPS) * gain


def _rope(x, cos_t, sin_t):
    width = x.shape[1]
    reps = width // LANES
    c = jnp.concatenate([cos_t] * reps, axis=1)
    s = jnp.concatenate([sin_t] * reps, axis=1)
    half = HEAD_DIM // 2
    lane = lax.broadcasted_iota(jnp.int32, x.shape, 1)
    fwd = pltpu.roll(x, width - half, axis=1)
    bwd = pltpu.roll(x, half, axis=1)
    partner = jnp.where((lane % HEAD_DIM) < half, fwd, bwd)
    return x * c + partner * s


def _swiglu_residual(x, gain, wg_ref, wu_ref, wd_ref):
    h = _rms_rows(x, gain).astype(BF16)
    gate = _dot(h, wg_ref[...])
    up = _dot(h, wu_ref[...])
    act = (gate * (1.0 / (1.0 + jnp.exp(-gate))) * up).astype(BF16)
    return x + 0.5 * _dot(act, wd_ref[...])


def _ffn_kernel(x_ref, g_ref, wg_ref, wu_ref, wd_ref, o_ref):
    o_ref[...] = _swiglu_residual(x_ref[...], g_ref[...], wg_ref, wu_ref, wd_ref)


def _out_ffn_kernel(x_ref, os_ref, om_ref, wos_ref, wom_ref, g_ref, wg_ref, wu_ref, wd_ref, o_ref):
    x = x_ref[...] + _dot(os_ref[...], wos_ref[...]) + _dot(om_ref[...], wom_ref[...])
    o_ref[...] = _swiglu_residual(x, g_ref[...], wg_ref, wu_ref, wd_ref)


def _resident(shape):
    return pl.BlockSpec(shape, lambda i: (0,) * len(shape), pipeline_mode=pl.Buffered(1))


def _layer_resident(layer, shape, row_block=0):
    return pl.BlockSpec((None,) + shape, lambda i: (layer, row_block) + (0,) * (len(shape) - 1),
                        pipeline_mode=pl.Buffered(1))


def _ffn_call(layer, x, gain, wg, wu, wd, mix=None):
    tokens = x.shape[0]
    rows = pl.BlockSpec((FFN_ROWS, D_MODEL), lambda i: (i, 0))
    ffn_specs = [_layer_resident(layer, (1, D_MODEL)), _layer_resident(layer, (D_MODEL, D_FF)),
                 _layer_resident(layer, (D_MODEL, D_FF)), _layer_resident(layer, (D_FF, D_MODEL))]
    if mix is None:
        body, ins, specs, name = _ffn_kernel, (x, gain, wg, wu, wd), [rows] + ffn_specs, "ffn"
    else:
        o_s, o_m, w_out = mix
        body, name = _out_ffn_kernel, "out_ffn"
        ins = (x, o_s, o_m, w_out, w_out, gain, wg, wu, wd)
        specs = [rows,
                 pl.BlockSpec((FFN_ROWS, SELF_WIDTH), lambda i: (i, 0)),
                 pl.BlockSpec((FFN_ROWS, MEM_WIDTH), lambda i: (i, 0)),
                 _layer_resident(layer, (SELF_WIDTH, D_MODEL)),
                 _layer_resident(layer, (MEM_WIDTH, D_MODEL), SELF_WIDTH // MEM_WIDTH)] + ffn_specs
    return pl.pallas_call(
        body,
        grid=(tokens // FFN_ROWS,),
        in_specs=specs,
        out_specs=rows,
        out_shape=jax.ShapeDtypeStruct((tokens, D_MODEL), F32),
        compiler_params=pltpu.CompilerParams(dimension_semantics=("arbitrary",),
                                             vmem_limit_bytes=VMEM_LIMIT),
        name=name,
    )(*ins)


def _mem_kv_kernel(mem_ref, g_ref, w_ref, gk_ref, ones_ref, k_ref, v_ref):
    h = _rms_rows(mem_ref[0], g_ref[0]).astype(BF16)
    kv = _dot(h, w_ref[0])
    k_ref[0, 0] = _head_rms(kv[:, :MEM_WIDTH], gk_ref[0], ones_ref[...]).astype(BF16)
    v_ref[0, 0] = kv[:, MEM_WIDTH:].astype(BF16)


def _mem_kv_call(mem, mem_norm, w_mem_kv, gk_tiled, seg_ones):
    batch = mem.shape[0]
    out = jax.ShapeDtypeStruct((DEPTH, batch, N_MEM, MEM_WIDTH), BF16)
    kv_spec = pl.BlockSpec((1, 1, N_MEM, MEM_WIDTH), lambda l, b: (l, b, 0, 0))
    return pl.pallas_call(
        _mem_kv_kernel,
        grid=(DEPTH, batch),
        in_specs=[pl.BlockSpec((1, N_MEM, D_MODEL), lambda l, b: (b, 0, 0)),
                  pl.BlockSpec((1, 1, D_MODEL), lambda l, b: (l, 0, 0)),
                  pl.BlockSpec((1, D_MODEL, 2 * MEM_WIDTH), lambda l, b: (l, 0, 0)),
                  pl.BlockSpec((1, 1, MEM_WIDTH), lambda l, b: (l, 0, 0)),
                  pl.BlockSpec((MXU_COLS, MXU_COLS), lambda l, b: (0, 0))],
        out_specs=[kv_spec, kv_spec],
        out_shape=[out, out],
        compiler_params=pltpu.CompilerParams(dimension_semantics=("arbitrary", "arbitrary")),
        name="mem_kv",
    )(mem, mem_norm, w_mem_kv, gk_tiled, seg_ones)


def _proj_kernel(use_rope, x_ref, g_ref, w_ref, gq_ref, gk_ref, gm_ref, ones_ref, km_ref, vm_ref,
                 *rest):
    if use_rope:
        cos_ref, sin_ref, q_ref, k_ref, v_ref, om_ref = rest
    else:
        q_ref, k_ref, v_ref, om_ref = rest
    seg_ones = ones_ref[...]
    h = _rms_rows(x_ref[...], g_ref[...]).astype(BF16)

    def section(first, width):
        return _dot(h, w_ref[:, first:first + width])

    scale = HEAD_DIM ** -0.5
    q = _head_rms(section(0, SELF_WIDTH), gq_ref[...], seg_ones)
    if use_rope:
        q = _rope(q, cos_ref[...], sin_ref[...])
    q_ref[...] = (q * scale).astype(BF16)
    k = _head_rms(section(SELF_WIDTH, SELF_WIDTH), gk_ref[...], seg_ones)
    if use_rope:
        k = _rope(k, cos_ref[...], sin_ref[...])
    k_ref[...] = k.astype(BF16)
    v_ref[...] = section(2 * SELF_WIDTH, SELF_WIDTH).astype(BF16)

    qm = (_head_rms(section(3 * SELF_WIDTH, MEM_WIDTH), gm_ref[...], seg_ones) * scale).astype(BF16)
    km = km_ref[0, 0]
    vm = vm_ref[0, 0]
    head_of_lane = lax.broadcasted_iota(jnp.int32, qm.shape, 1) // HEAD_DIM
    om = jnp.zeros(qm.shape, F32)
    for hd in range(N_MEM_HEADS):
        mine = head_of_lane == hd
        s = _dot_nt(jnp.where(mine, qm, jnp.zeros_like(qm)), km)
        e = jnp.exp(s - jnp.max(s, axis=-1, keepdims=True))
        inv = 1.0 / jnp.sum(e, axis=-1, keepdims=True)
        om = jnp.where(mine, _dot(e.astype(BF16), vm) * inv, om)
    om_ref[...] = om.astype(BF16)


def _proj_call(layer, x, gain, w_in, gq, gk, gm, seg_ones, mem_k, mem_v, rope, seq):
    tokens = x.shape[0]
    per_seq = seq // PROJ_ROWS
    rows = lambda w: pl.BlockSpec((PROJ_ROWS, w), lambda i: (i, 0))
    mem_spec = pl.BlockSpec((1, 1, N_MEM, MEM_WIDTH), lambda i: (layer, i // per_seq, 0, 0))
    specs = [rows(D_MODEL), _layer_resident(layer, (1, D_MODEL)), _layer_resident(layer, (D_MODEL, IN_WIDTH)),
             _resident((1, SELF_WIDTH)), _resident((1, SELF_WIDTH)), _resident((1, MEM_WIDTH)),
             _resident((MXU_COLS, MXU_COLS)), mem_spec, mem_spec]
    ins = [x, gain, w_in, gq, gk, gm, seg_ones, mem_k, mem_v]
    if rope is not None:
        specs += [rows(LANES), rows(LANES)]
        ins += list(rope)
    wide = jax.ShapeDtypeStruct((tokens, SELF_WIDTH), BF16)
    return pl.pallas_call(
        functools.partial(_proj_kernel, rope is not None),
        grid=(tokens // PROJ_ROWS,),
        in_specs=specs,
        out_specs=[rows(SELF_WIDTH), rows(SELF_WIDTH), rows(SELF_WIDTH), rows(MEM_WIDTH)],
        out_shape=[wide, wide, wide, jax.ShapeDtypeStruct((tokens, MEM_WIDTH), BF16)],
        compiler_params=pltpu.CompilerParams(dimension_semantics=("arbitrary",),
                                             vmem_limit_bytes=VMEM_LIMIT),
        name="proj_rope" if rope is not None else "proj",
    )(*ins)


def _chunk_attn_kernel(q_ref, k_ref, v_ref, ramp_ref, o_ref, tab_ref, kpad, vpad, s_a, s_b, p_a, p_b):
    seq = q_ref.shape[1]
    n_blocks = seq // A_ROWS
    n_head = A_PAD // A_ROWS
    rows = 2 * A_ROWS

    row_chunk = lax.broadcasted_iota(jnp.int32, (A_ROWS, A_BAND), 0) // CHUNK
    col_chunk = lax.broadcasted_iota(jnp.int32, (A_ROWS, A_BAND), 1) // CHUNK
    allowed = (col_chunk >= row_chunk) & (col_chunk <= row_chunk + LEFT_CHUNKS)
    for hd in range(LANES // HEAD_DIM):
        ramp = jnp.broadcast_to(ramp_ref[hd], (A_ROWS, A_RAMP))
        toeplitz = pltpu.roll(ramp, A_RAMP - A_ROWS, axis=1, stride=1, stride_axis=0)
        tab_ref[hd * A_ROWS:(hd + 1) * A_ROWS, :] = jnp.where(allowed, toeplitz[:, :A_BAND], NEG_INF)

    kpad[:A_PAD, :] = jnp.zeros((A_PAD, LANES), BF16)
    vpad[:A_PAD, :] = jnp.zeros((A_PAD, LANES), BF16)
    kpad[A_PAD:, :] = k_ref[0, :n_head * A_ROWS, :]
    vpad[A_PAD:, :] = v_ref[0, :n_head * A_ROWS, :]

    def band(ref, pad_ref, t):
        if isinstance(t, int) and t < n_head:
            return pad_ref[t * A_ROWS:t * A_ROWS + A_BAND, :]
        return ref[0, pl.ds(pl.multiple_of((t - n_head) * A_ROWS, A_ROWS), A_BAND), :]

    def block_rows(t):
        return pl.ds(pl.multiple_of(t * A_ROWS, A_ROWS), A_ROWS)

    lane = lax.broadcasted_iota(jnp.int32, (A_ROWS, LANES), 1)

    def scores(t, s_ref):
        q = q_ref[0, block_rows(t), :]
        stacked = jnp.concatenate([jnp.where(lane < HEAD_DIM, q, jnp.zeros_like(q)),
                                   jnp.where(lane >= HEAD_DIM, q, jnp.zeros_like(q))], axis=0)
        s_ref[...] = _dot_nt(stacked, band(k_ref, kpad, t))

    def update(t, s_ref, p_ref, first_slot):
        inv = []
        for c in range(rows // A_SUB):
            r = slice(c * A_SUB, (c + 1) * A_SUB)
            s = s_ref[r, :] + tab_ref[r, :]
            if first_slot:
                slot = lax.broadcasted_iota(jnp.int32, s.shape, 1)
                s = jnp.where(slot >= first_slot, s, NEG_INF)
            e = jnp.exp(s - jnp.max(s, axis=-1, keepdims=True))
            part = e[:, :LANES]
            for g in range(1, A_BAND // LANES):
                part = part + e[:, g * LANES:(g + 1) * LANES]
            inv.append(1.0 / jnp.sum(part, axis=-1, keepdims=True))
            p_ref[r, :] = e.astype(BF16)
        pv = _dot(p_ref[...], band(v_ref, vpad, t)) * jnp.concatenate(inv, axis=0)
        o_ref[0, block_rows(t), :] = jnp.where(lane < HEAD_DIM, pv[:A_ROWS], pv[A_ROWS:]).astype(BF16)

    scores(0, s_a)
    scores(1, s_b)
    update(0, s_a, p_a, A_PAD)
    scores(2, s_a)
    update(1, s_b, p_b, A_PAD - A_ROWS)

    def two_blocks(u, carry):
        t = 2 * u
        scores(t + 1, s_b)
        update(t, s_a, p_a, 0)
        scores(jnp.minimum(t + 2, n_blocks - 1), s_a)
        update(t + 1, s_b, p_b, 0)
        return carry

    lax.fori_loop(1, n_blocks // 2, two_blocks, 0)


def _chunk_attn_call(q, k, v, ramp):
    batch, seq, _ = q.shape
    rows = 2 * A_ROWS
    heads_per = LANES // HEAD_DIM
    n_head = A_PAD // A_ROWS
    seq_spec = pl.BlockSpec((1, seq, LANES), lambda b, p: (b, 0, p))
    return pl.pallas_call(
        _chunk_attn_kernel,
        grid=(batch, SELF_WIDTH // LANES),
        in_specs=[seq_spec, seq_spec, seq_spec,
                  pl.BlockSpec((heads_per, 1, A_RAMP), lambda b, p: (p, 0, 0))],
        out_specs=seq_spec,
        out_shape=jax.ShapeDtypeStruct((batch, seq, SELF_WIDTH), BF16),
        scratch_shapes=[pltpu.VMEM((rows, A_BAND), F32),
                        pltpu.VMEM((A_PAD + n_head * A_ROWS, LANES), BF16),
                        pltpu.VMEM((A_PAD + n_head * A_ROWS, LANES), BF16),
                        pltpu.VMEM((rows, A_BAND), F32), pltpu.VMEM((rows, A_BAND), F32),
                        pltpu.VMEM((rows, A_BAND), BF16), pltpu.VMEM((rows, A_BAND), BF16)],
        compiler_params=pltpu.CompilerParams(dimension_semantics=("arbitrary", "arbitrary")),
        name="chunk_attn",
    )(q, k, v, ramp)


def _chunk_bias_ramp(rel_bias):
    heads = rel_bias.shape[0]
    n_far = A_PAD + A_ROWS - REL_CLIP
    n_near = A_RAMP - n_far - (2 * REL_CLIP + 1)
    ramp = jnp.concatenate([jnp.broadcast_to(rel_bias[:, -1:], (heads, n_far)), rel_bias[:, ::-1],
                            jnp.broadcast_to(rel_bias[:, :1], (heads, n_near))], axis=1)
    return ramp.astype(F32).reshape(heads, 1, A_RAMP)


def _diff_attn_kernel(lambda_init, q_ref, k_ref, v_ref, lq1_ref, lk1_ref, lq2_ref, lk2_ref,
                      gain_ref, o_ref, qs_ref, s_a, s_b, p_a, p_b, m_ref, l_ref, acc_ref):
    rows = 2 * B_ROWS
    seq = q_ref.shape[1]

    def begin(j):
        q = q_ref[0, key_rows(j), :]
        lane = lax.broadcasted_iota(jnp.int32, q.shape, 1)
        qs_ref[:B_ROWS, :] = jnp.where(lane < HEAD_DIM, q, jnp.zeros_like(q))
        qs_ref[B_ROWS:, :] = jnp.where(lane >= HEAD_DIM, q, jnp.zeros_like(q))
        m_ref[...] = jnp.full(m_ref.shape, NEG_INF, F32)
        l_ref[...] = jnp.zeros(l_ref.shape, F32)
        acc_ref[...] = jnp.zeros(acc_ref.shape, F32)

    def key_rows(t):
        return pl.ds(pl.multiple_of(t * B_ROWS, B_ROWS), B_ROWS)

    def scores(t, s_ref):
        s_ref[...] = _dot_nt(qs_ref[...], k_ref[0, key_rows(t), :])

    def update(t, s_ref, p_ref, diagonal):
        for c in range(rows // B_SUB):
            r = slice(c * B_SUB, (c + 1) * B_SUB)
            s = s_ref[r, :]
            if diagonal:
                base = (c * B_SUB) % B_ROWS
                row_chunk = (lax.broadcasted_iota(jnp.int32, s.shape, 0) + base) // CHUNK
                col_chunk = lax.broadcasted_iota(jnp.int32, s.shape, 1) // CHUNK
                s = jnp.where(col_chunk <= row_chunk, s, NEG_INF)
            m_old = m_ref[r, :]
            m_new = jnp.maximum(m_old, jnp.max(s, axis=-1, keepdims=True))
            alpha = jnp.exp(m_old - m_new)
            e = jnp.exp(s - m_new)
            part = e[:, :LANES]
            for g in range(1, B_ROWS // LANES):
                part = part + e[:, g * LANES:(g + 1) * LANES]
            m_ref[r, :] = m_new
            l_ref[r, :] = alpha * l_ref[r, :] + part
            acc_ref[r, :] = alpha * acc_ref[r, :]
            p_ref[r, :] = e.astype(BF16)
        acc_ref[...] += _dot(p_ref[...], v_ref[0, key_rows(t), :])

    bufs = ((s_a, p_a), (s_b, p_b))

    def full_blocks(first, count):
        for i in range(count):
            scores(first + i + 1, bufs[(i + 1) % 2][0])
            update(first + i, *bufs[i % 2], False)

    def block_group(u, carry):
        full_blocks(B_GROUP * u, B_GROUP)
        return carry

    lam = (jnp.exp(jnp.sum(lq1_ref[...] * lk1_ref[...], axis=-1, keepdims=True))
           - jnp.exp(jnp.sum(lq2_ref[...] * lk2_ref[...], axis=-1, keepdims=True)) + lambda_init)

    def query_block(j, carry):
        n_full = j
        begin(j)
        scores(0, s_a)
        lax.fori_loop(0, n_full // B_GROUP, block_group, 0)
        left = n_full % B_GROUP
        for count in range(B_GROUP):
            @pl.when(left == count)
            def _(count=count):
                full_blocks(n_full - count, count)
                update(n_full, *bufs[count % 2], True)

        inv_l = 1.0 / jnp.sum(l_ref[...], axis=-1, keepdims=True)
        o_all = acc_ref[...] * inv_l
        o = o_all[:B_ROWS] - lam * o_all[B_ROWS:]
        o_ref[0, key_rows(j), :] = (_rms_rows(o, gain_ref[...]) * (1.0 - lambda_init)).astype(BF16)
        return carry

    lax.fori_loop(0, seq // B_ROWS, query_block, 0)


def _diff_attn_call(q, k, v, lq1, lk1, lq2, lk2, gain, lambda_init):
    batch, seq, _ = q.shape
    rows = 2 * B_ROWS
    seq_spec = pl.BlockSpec((1, seq, LANES), lambda b, h: (b, 0, h))
    vec = lambda w: pl.BlockSpec((1, w), lambda b, h: (0, 0))
    return pl.pallas_call(
        functools.partial(_diff_attn_kernel, lambda_init),
        grid=(batch, N_DIFF_HEADS),
        in_specs=[seq_spec, seq_spec, seq_spec, vec(HEAD_DIM), vec(HEAD_DIM), vec(HEAD_DIM),
                  vec(HEAD_DIM), vec(2 * HEAD_DIM)],
        out_specs=seq_spec,
        out_shape=jax.ShapeDtypeStruct((batch, seq, SELF_WIDTH), BF16),
        scratch_shapes=[pltpu.VMEM((rows, LANES), BF16),
                        pltpu.VMEM((rows, B_ROWS), F32), pltpu.VMEM((rows, B_ROWS), F32),
                        pltpu.VMEM((rows, B_ROWS), BF16), pltpu.VMEM((rows, B_ROWS), BF16),
                        pltpu.VMEM((rows, 1), F32), pltpu.VMEM((rows, LANES), F32),
                        pltpu.VMEM((rows, LANES), F32)],
        compiler_params=pltpu.CompilerParams(dimension_semantics=("arbitrary", "arbitrary")),
        name="diff_attn",
    )(q, k, v, lq1, lk1, lq2, lk2, gain)


def _rope_tables(positions):
    inv_freq = 1.0 / (ROPE_THETA ** (jnp.arange(0, HEAD_DIM, 2, dtype=F32) / HEAD_DIM))
    ang = positions.astype(F32).reshape(-1, 1) * inv_freq
    cos, sin = jnp.cos(ang), jnp.sin(ang)
    reps = LANES // HEAD_DIM
    return (jnp.tile(jnp.concatenate([cos, cos], axis=1), (1, reps)),
            jnp.tile(jnp.concatenate([-sin, sin], axis=1), (1, reps)))


def kernel(x, mem, positions, ffn1_norm, ffn1_w_gate, ffn1_w_up, ffn1_w_down, mix_norm, mem_norm, w_in, w_mem_kv, mem_q_norm, mem_k_norm, w_out, a_q_norm, a_k_norm, a_rel_bias, b_q_norm, b_k_norm, b_lambda_q1, b_lambda_k1, b_lambda_q2, b_lambda_k2, b_subln, ffn2_norm, ffn2_w_gate, ffn2_w_up, ffn2_w_down):
    batch, seq, _ = x.shape
    tokens = batch * seq
    row = lambda a: a.reshape(1, -1).astype(F32)
    tiled = lambda a, heads: jnp.tile(a.astype(F32), heads).reshape(1, -1)
    bf = lambda a: a.astype(BF16)

    seg = jnp.arange(MXU_COLS) // HEAD_DIM
    seg_ones = (seg[:, None] == seg[None, :]).astype(BF16)
    rope = _rope_tables(positions)

    gk_mem = jnp.stack([tiled(mem_k_norm[i], N_MEM_HEADS) for i in range(DEPTH)])
    mem_k, mem_v = _mem_kv_call(mem, mem_norm.reshape(DEPTH, 1, D_MODEL), bf(w_mem_kv), gk_mem, seg_ones)

    stacked_gain = lambda a: a.reshape(DEPTH, 1, D_MODEL).astype(F32)
    ffn1 = (stacked_gain(ffn1_norm), bf(ffn1_w_gate), bf(ffn1_w_up), bf(ffn1_w_down))
    ffn2 = (stacked_gain(ffn2_norm), bf(ffn2_w_gate), bf(ffn2_w_up), bf(ffn2_w_down))
    mix_gain, w_in_bf, w_out_bf = stacked_gain(mix_norm), bf(w_in), bf(w_out)

    xf = x.reshape(tokens, D_MODEL)
    for i in range(DEPTH):
        j = i // N_MIXERS
        mixer_a = i % N_MIXERS == 0
        xf = _ffn_call(i, xf, *ffn1)
        gq, gk = (a_q_norm[j], a_k_norm[j]) if mixer_a else (b_q_norm[j], b_k_norm[j])
        q, k, v, o_m = _proj_call(i, xf, mix_gain, w_in_bf, tiled(gq, N_SELF_HEADS),
                                  tiled(gk, N_SELF_HEADS), tiled(mem_q_norm[i], N_MEM_HEADS),
                                  seg_ones, mem_k, mem_v, None if mixer_a else rope, seq)
        q, k, v = (a.reshape(batch, seq, SELF_WIDTH) for a in (q, k, v))
        if mixer_a:
            o_s = _chunk_attn_call(q, k, v, _chunk_bias_ramp(a_rel_bias[j]))
        else:
            lambda_init = 0.8 - 0.6 * math.exp(-0.3 * i)
            o_s = _diff_attn_call(q, k, v, row(b_lambda_q1[j]), row(b_lambda_k1[j]), row(b_lambda_q2[j]),
                                  row(b_lambda_k2[j]), row(b_subln[j]), lambda_init)
        xf = _ffn_call(i, xf, *ffn2, mix=(o_s.reshape(tokens, SELF_WIDTH), o_m, w_out_bf))
    return xf.reshape(batch, seq, D_MODEL)
```

```python
import functools
import math

import jax
import jax.numpy as jnp
from jax import lax
from jax.experimental import pallas as pl
from jax.experimental.pallas import tpu as pltpu

D_MODEL = 1024
DEPTH = 4
CHUNK = 64
HEAD_DIM = 64
N_SELF_HEADS = 12
N_DIFF_HEADS = 6
N_MEM_HEADS = 4
N_MEM = 256
SELF_WIDTH = N_SELF_HEADS * HEAD_DIM
MEM_WIDTH = N_MEM_HEADS * HEAD_DIM
IN_WIDTH = 3 * SELF_WIDTH + MEM_WIDTH
LEFT_CHUNKS = 8
REL_CLIP = 128
D_FF = 2816
ROPE_THETA = 10000.0
EPS = 1e-6
NEG_INF = -1e30
N_MIXERS = 2

LANES = 128
MXU_COLS = 256
VMEM_LIMIT = 56 * 1024 * 1024

FFN_ROWS = 512
PROJ_ROWS = 512
A_ROWS = 4 * CHUNK
A_PAD = LEFT_CHUNKS * CHUNK
A_BAND = A_ROWS + A_PAD
A_RAMP = A_BAND + A_ROWS
A_SUB = 128
B_ROWS = 512
B_SUB = 64
B_GROUP = 2

F32 = jnp.float32
BF16 = jnp.bfloat16


def _dot(a, b):
    return jnp.dot(a, b, preferred_element_type=F32)


def _dot_nt(a, b):
    return lax.dot_general(a, b, (((1,), (1,)), ((), ())), preferred_element_type=F32)


def _rms_rows(x, gain):
    ms = jnp.mean(x * x, axis=-1, keepdims=True)
    return x * lax.rsqrt(ms + EPS) * gain


def _head_rms(x, gain, seg_ones):
    x2 = x * x
    hi = x2.astype(BF16)
    lo = (x2 - hi.astype(F32)).astype(BF16)
    parts = []
    for g in range(x.shape[1] // MXU_COLS):
        sl = slice(g * MXU_COLS, (g + 1) * MXU_COLS)
        parts.append(_dot(hi[:, sl], seg_ones) + _dot(lo[:, sl], seg_ones))
    ss = parts[0] if len(parts) == 1 else jnp.concatenate(parts, axis=1)
    return x * lax.rsqrt(ss * (1.0 / HEAD_DIM) + EPS) * gain


def _rope(x, cos_t, sin_t):
    width = x.shape[1]
    reps = width // LANES
    c = jnp.concatenate([cos_t] * reps, axis=1)
    s = jnp.concatenate([sin_t] * reps, axis=1)
    half = HEAD_DIM // 2
    lane = lax.broadcasted_iota(jnp.int32, x.shape, 1)
    fwd = pltpu.roll(x, width - half, axis=1)
    bwd = pltpu.roll(x, half, axis=1)
    partner = jnp.where((lane % HEAD_DIM) < half, fwd, bwd)
    return x * c + partner * s


def _swiglu_residual(x, gain, wg_ref, wu_ref, wd_ref):
    h = _rms_rows(x, gain).astype(BF16)
    gate = _dot(h, wg_ref[...])
    up = _dot(h, wu_ref[...])
    act = (gate * (1.0 / (1.0 + jnp.exp(-gate))) * up).astype(BF16)
    return x + 0.5 * _dot(act, wd_ref[...])


def _ffn_kernel(x_ref, g_ref, wg_ref, wu_ref, wd_ref, o_ref):
    o_ref[...] = _swiglu_residual(x_ref[...], g_ref[...], wg_ref, wu_ref, wd_ref)


def _out_ffn_kernel(x_ref, os_ref, om_ref, wos_ref, wom_ref, g_ref, wg_ref, wu_ref, wd_ref, o_ref):
    x = x_ref[...] + _dot(os_ref[...], wos_ref[...]) + _dot(om_ref[...], wom_ref[...])
    o_ref[...] = _swiglu_residual(x, g_ref[...], wg_ref, wu_ref, wd_ref)


def _resident(shape):
    return pl.BlockSpec(shape, lambda i: (0,) * len(shape), pipeline_mode=pl.Buffered(1))


def _layer_resident(layer, shape, row_block=0):
    return pl.BlockSpec((None,) + shape, lambda i: (layer, row_block) + (0,) * (len(shape) - 1),
                        pipeline_mode=pl.Buffered(1))


def _ffn_call(layer, x, gain, wg, wu, wd, mix=None):
    tokens = x.shape[0]
    rows = pl.BlockSpec((FFN_ROWS, D_MODEL), lambda i: (i, 0))
    ffn_specs = [_layer_resident(layer, (1, D_MODEL)), _layer_resident(layer, (D_MODEL, D_FF)),
                 _layer_resident(layer, (D_MODEL, D_FF)), _layer_resident(layer, (D_FF, D_MODEL))]
    if mix is None:
        body, ins, specs, name = _ffn_kernel, (x, gain, wg, wu, wd), [rows] + ffn_specs, "ffn"
    else:
        o_s, o_m, w_out = mix
        body, name = _out_ffn_kernel, "out_ffn"
        ins = (x, o_s, o_m, w_out, w_out, gain, wg, wu, wd)
        specs = [rows,
                 pl.BlockSpec((FFN_ROWS, SELF_WIDTH), lambda i: (i, 0)),
                 pl.BlockSpec((FFN_ROWS, MEM_WIDTH), lambda i: (i, 0)),
                 _layer_resident(layer, (SELF_WIDTH, D_MODEL)),
                 _layer_resident(layer, (MEM_WIDTH, D_MODEL), SELF_WIDTH // MEM_WIDTH)] + ffn_specs
    return pl.pallas_call(
        body,
        grid=(tokens // FFN_ROWS,),
        in_specs=specs,
        out_specs=rows,
        out_shape=jax.ShapeDtypeStruct((tokens, D_MODEL), F32),
        compiler_params=pltpu.CompilerParams(dimension_semantics=("arbitrary",),
                                             vmem_limit_bytes=VMEM_LIMIT),
        name=name,
    )(*ins)


def _mem_kv_kernel(mem_ref, g_ref, w_ref, gk_ref, ones_ref, k_ref, v_ref):
    h = _rms_rows(mem_ref[0], g_ref[0]).astype(BF16)
    kv = _dot(h, w_ref[0])
    k_ref[0, 0] = _head_rms(kv[:, :MEM_WIDTH], gk_ref[0], ones_ref[...]).astype(BF16)
    v_ref[0, 0] = kv[:, MEM_WIDTH:].astype(BF16)


def _mem_kv_call(mem, mem_norm, w_mem_kv, gk_tiled, seg_ones):
    batch = mem.shape[0]
    out = jax.ShapeDtypeStruct((DEPTH, batch, N_MEM, MEM_WIDTH), BF16)
    kv_spec = pl.BlockSpec((1, 1, N_MEM, MEM_WIDTH), lambda l, b: (l, b, 0, 0))
    return pl.pallas_call(
        _mem_kv_kernel,
        grid=(DEPTH, batch),
        in_specs=[pl.BlockSpec((1, N_MEM, D_MODEL), lambda l, b: (b, 0, 0)),
                  pl.BlockSpec((1, 1, D_MODEL), lambda l, b: (l, 0, 0)),
                  pl.BlockSpec((1, D_MODEL, 2 * MEM_WIDTH), lambda l, b: (l, 0, 0)),
                  pl.BlockSpec((1, 1, MEM_WIDTH), lambda l, b: (l, 0, 0)),
                  pl.BlockSpec((MXU_COLS, MXU_COLS), lambda l, b: (0, 0))],
        out_specs=[kv_spec, kv_spec],
        out_shape=[out, out],
        compiler_params=pltpu.CompilerParams(dimension_semantics=("arbitrary", "arbitrary")),
        name="mem_kv",
    )(mem, mem_norm, w_mem_kv, gk_tiled, seg_ones)


def _proj_kernel(use_rope, x_ref, g_ref, w_ref, gq_ref, gk_ref, gm_ref, ones_ref, km_ref, vm_ref,
                 *rest):
    if use_rope:
        cos_ref, sin_ref, q_ref, k_ref, v_ref, om_ref = rest
    else:
        q_ref, k_ref, v_ref, om_ref = rest
    seg_ones = ones_ref[...]
    h = _rms_rows(x_ref[...], g_ref[...]).astype(BF16)

    def section(first, width):
        return _dot(h, w_ref[:, first:first + width])

    scale = HEAD_DIM ** -0.5
    q = _head_rms(section(0, SELF_WIDTH), gq_ref[...], seg_ones)
    if use_rope:
        q = _rope(q, cos_ref[...], sin_ref[...])
    q_ref[...] = (q * scale).astype(BF16)
    k = _head_rms(section(SELF_WIDTH, SELF_WIDTH), gk_ref[...], seg_ones)
    if use_rope:
        k = _rope(k, cos_ref[...], sin_ref[...])
    k_ref[...] = k.astype(BF16)
    v_ref[...] = section(2 * SELF_WIDTH, SELF_WIDTH).astype(BF16)

    qm = (_head_rms(section(3 * SELF_WIDTH, MEM_WIDTH), gm_ref[...], seg_ones) * scale).astype(BF16)
    km = km_ref[0, 0]
    vm = vm_ref[0, 0]
    head_of_lane = lax.broadcasted_iota(jnp.int32, qm.shape, 1) // HEAD_DIM
    om = jnp.zeros(qm.shape, F32)
    for hd in range(N_MEM_HEADS):
        mine = head_of_lane == hd
        s = _dot_nt(jnp.where(mine, qm, jnp.zeros_like(qm)), km)
        e = jnp.exp(s - jnp.max(s, axis=-1, keepdims=True))
        inv = 1.0 / jnp.sum(e, axis=-1, keepdims=True)
        om = jnp.where(mine, _dot(e.astype(BF16), vm) * inv, om)
    om_ref[...] = om.astype(BF16)


def _proj_call(layer, x, gain, w_in, gq, gk, gm, seg_ones, mem_k, mem_v, rope, seq):
    tokens = x.shape[0]
    per_seq = seq // PROJ_ROWS
    rows = lambda w: pl.BlockSpec((PROJ_ROWS, w), lambda i: (i, 0))
    mem_spec = pl.BlockSpec((1, 1, N_MEM, MEM_WIDTH), lambda i: (layer, i // per_seq, 0, 0))
    specs = [rows(D_MODEL), _layer_resident(layer, (1, D_MODEL)), _layer_resident(layer, (D_MODEL, IN_WIDTH)),
             _resident((1, SELF_WIDTH)), _resident((1, SELF_WIDTH)), _resident((1, MEM_WIDTH)),
             _resident((MXU_COLS, MXU_COLS)), mem_spec, mem_spec]
    ins = [x, gain, w_in, gq, gk, gm, seg_ones, mem_k, mem_v]
    if rope is not None:
        specs += [rows(LANES), rows(LANES)]
        ins += list(rope)
    wide = jax.ShapeDtypeStruct((tokens, SELF_WIDTH), BF16)
    return pl.pallas_call(
        functools.partial(_proj_kernel, rope is not None),
        grid=(tokens // PROJ_ROWS,),
        in_specs=specs,
        out_specs=[rows(SELF_WIDTH), rows(SELF_WIDTH), rows(SELF_WIDTH), rows(MEM_WIDTH)],
        out_shape=[wide, wide, wide, jax.ShapeDtypeStruct((tokens, MEM_WIDTH), BF16)],
        compiler_params=pltpu.CompilerParams(dimension_semantics=("arbitrary",),
                                             vmem_limit_bytes=VMEM_LIMIT),
        name="proj_rope" if rope is not None else "proj",
    )(*ins)


def _chunk_attn_kernel(q_ref, k_ref, v_ref, ramp_ref, o_ref, tab_ref, kpad, vpad, s_a, s_b, p_a, p_b):
    seq = q_ref.shape[1]
    n_blocks = seq // A_ROWS
    n_head = A_PAD // A_ROWS
    rows = 2 * A_ROWS

    row_chunk = lax.broadcasted_iota(jnp.int32, (A_ROWS, A_BAND), 0) // CHUNK
    col_chunk = lax.broadcasted_iota(jnp.int32, (A_ROWS, A_BAND), 1) // CHUNK
    allowed = (col_chunk >= row_chunk) & (col_chunk <= row_chunk + LEFT_CHUNKS)
    for hd in range(LANES // HEAD_DIM):
        ramp = jnp.broadcast_to(ramp_ref[hd], (A_ROWS, A_RAMP))
        toeplitz = pltpu.roll(ramp, A_RAMP - A_ROWS, axis=1, stride=1, stride_axis=0)
        tab_ref[hd * A_ROWS:(hd + 1) * A_ROWS, :] = jnp.where(allowed, toeplitz[:, :A_BAND], NEG_INF)

    kpad[:A_PAD, :] = jnp.zeros((A_PAD, LANES), BF16)
    vpad[:A_PAD, :] = jnp.zeros((A_PAD, LANES), BF16)
    kpad[A_PAD:, :] = k_ref[0, :n_head * A_ROWS, :]
    vpad[A_PAD:, :] = v_ref[0, :n_head * A_ROWS, :]

    def band(ref, pad_ref, t):
        if isinstance(t, int) and t < n_head:
            return pad_ref[t * A_ROWS:t * A_ROWS + A_BAND, :]
        return ref[0, pl.ds(pl.multiple_of((t - n_head) * A_ROWS, A_ROWS), A_BAND), :]

    def block_rows(t):
        return pl.ds(pl.multiple_of(t * A_ROWS, A_ROWS), A_ROWS)

    lane = lax.broadcasted_iota(jnp.int32, (A_ROWS, LANES), 1)

    def scores(t, s_ref):
        q = q_ref[0, block_rows(t), :]
        stacked = jnp.concatenate([jnp.where(lane < HEAD_DIM, q, jnp.zeros_like(q)),
                                   jnp.where(lane >= HEAD_DIM, q, jnp.zeros_like(q))], axis=0)
        s_ref[...] = _dot_nt(stacked, band(k_ref, kpad, t))

    def update(t, s_ref, p_ref, first_slot):
        inv = []
        for c in range(rows // A_SUB):
            r = slice(c * A_SUB, (c + 1) * A_SUB)
            first_chunk = ((c * A_SUB) % A_ROWS) // CHUNK
            last_chunk = ((c * A_SUB) % A_ROWS + A_SUB - 1) // CHUNK
            lo = first_chunk * CHUNK // LANES * LANES
            hi = -(-((last_chunk + LEFT_CHUNKS + 1) * CHUNK) // LANES) * LANES
            s = s_ref[r, lo:hi] + tab_ref[r, lo:hi]
            if first_slot:
                slot = lax.broadcasted_iota(jnp.int32, s.shape, 1) + lo
                s = jnp.where(slot >= first_slot, s, NEG_INF)
            e = jnp.exp(s - jnp.max(s, axis=-1, keepdims=True))
            part = e[:, :LANES]
            for g in range(1, (hi - lo) // LANES):
                part = part + e[:, g * LANES:(g + 1) * LANES]
            inv.append(1.0 / jnp.sum(part, axis=-1, keepdims=True))
            p_ref[r, lo:hi] = e.astype(BF16)
            if lo > 0:
                p_ref[r, :lo] = jnp.zeros((A_SUB, lo), BF16)
            if hi < A_BAND:
                p_ref[r, hi:] = jnp.zeros((A_SUB, A_BAND - hi), BF16)
        pv = _dot(p_ref[...], band(v_ref, vpad, t)) * jnp.concatenate(inv, axis=0)
        o_ref[0, block_rows(t), :] = jnp.where(lane < HEAD_DIM, pv[:A_ROWS], pv[A_ROWS:]).astype(BF16)

    scores(0, s_a)
    scores(1, s_b)
    update(0, s_a, p_a, A_PAD)
    scores(2, s_a)
    update(1, s_b, p_b, A_PAD - A_ROWS)

    def two_blocks(u, carry):
        t = 2 * u
        scores(t + 1, s_b)
        update(t, s_a, p_a, 0)
        scores(jnp.minimum(t + 2, n_blocks - 1), s_a)
        update(t + 1, s_b, p_b, 0)
        return carry

    lax.fori_loop(1, n_blocks // 2, two_blocks, 0)


def _chunk_attn_call(q, k, v, ramp):
    batch, seq, _ = q.shape
    rows = 2 * A_ROWS
    heads_per = LANES // HEAD_DIM
    n_head = A_PAD // A_ROWS
    seq_spec = pl.BlockSpec((1, seq, LANES), lambda b, p: (b, 0, p))
    return pl.pallas_call(
        _chunk_attn_kernel,
        grid=(batch, SELF_WIDTH // LANES),
        in_specs=[seq_spec, seq_spec, seq_spec,
                  pl.BlockSpec((heads_per, 1, A_RAMP), lambda b, p: (p, 0, 0))],
        out_specs=seq_spec,
        out_shape=jax.ShapeDtypeStruct((batch, seq, SELF_WIDTH), BF16),
        scratch_shapes=[pltpu.VMEM((rows, A_BAND), F32),
                        pltpu.VMEM((A_PAD + n_head * A_ROWS, LANES), BF16),
                        pltpu.VMEM((A_PAD + n_head * A_ROWS, LANES), BF16),
                        pltpu.VMEM((rows, A_BAND), F32), pltpu.VMEM((rows, A_BAND), F32),
                        pltpu.VMEM((rows, A_BAND), BF16), pltpu.VMEM((rows, A_BAND), BF16)],
        compiler_params=pltpu.CompilerParams(dimension_semantics=("arbitrary", "arbitrary")),
        name="chunk_attn",
    )(q, k, v, ramp)


def _chunk_bias_ramp(rel_bias):
    heads = rel_bias.shape[0]
    n_far = A_PAD + A_ROWS - REL_CLIP
    n_near = A_RAMP - n_far - (2 * REL_CLIP + 1)
    ramp = jnp.concatenate([jnp.broadcast_to(rel_bias[:, -1:], (heads, n_far)), rel_bias[:, ::-1],
                            jnp.broadcast_to(rel_bias[:, :1], (heads, n_near))], axis=1)
    return ramp.astype(F32).reshape(heads, 1, A_RAMP)


def _diff_attn_kernel(lambda_init, q_ref, k_ref, v_ref, lq1_ref, lk1_ref, lq2_ref, lk2_ref,
                      gain_ref, o_ref, qs_ref, s_a, s_b, p_a, p_b, m_ref, l_ref, acc_ref):
    rows = 2 * B_ROWS
    seq = q_ref.shape[1]

    def begin(j):
        q = q_ref[0, key_rows(j), :]
        lane = lax.broadcasted_iota(jnp.int32, q.shape, 1)
        qs_ref[:B_ROWS, :] = jnp.where(lane < HEAD_DIM, q, jnp.zeros_like(q))
        qs_ref[B_ROWS:, :] = jnp.where(lane >= HEAD_DIM, q, jnp.zeros_like(q))
        m_ref[...] = jnp.full(m_ref.shape, NEG_INF, F32)
        l_ref[...] = jnp.zeros(l_ref.shape, F32)
        acc_ref[...] = jnp.zeros(acc_ref.shape, F32)

    def key_rows(t):
        return pl.ds(pl.multiple_of(t * B_ROWS, B_ROWS), B_ROWS)

    def scores(t, s_ref):
        s_ref[...] = _dot_nt(qs_ref[...], k_ref[0, key_rows(t), :])

    def update(t, s_ref, p_ref, diagonal):
        for c in range(rows // B_SUB):
            r = slice(c * B_SUB, (c + 1) * B_SUB)
            s = s_ref[r, :]
            if diagonal:
                base = (c * B_SUB) % B_ROWS
                row_chunk = (lax.broadcasted_iota(jnp.int32, s.shape, 0) + base) // CHUNK
                col_chunk = lax.broadcasted_iota(jnp.int32, s.shape, 1) // CHUNK
                s = jnp.where(col_chunk <= row_chunk, s, NEG_INF)
            m_old = m_ref[r, :]
            m_new = jnp.maximum(m_old, jnp.max(s, axis=-1, keepdims=True))
            alpha = jnp.exp(m_old - m_new)
            e = jnp.exp(s - m_new)
            part = e[:, :LANES]
            for g in range(1, B_ROWS // LANES):
                part = part + e[:, g * LANES:(g + 1) * LANES]
            m_ref[r, :] = m_new
            l_ref[r, :] = alpha * l_ref[r, :] + part
            acc_ref[r, :] = alpha * acc_ref[r, :]
            p_ref[r, :] = e.astype(BF16)
        acc_ref[...] += _dot(p_ref[...], v_ref[0, key_rows(t), :])

    bufs = ((s_a, p_a), (s_b, p_b))

    def full_blocks(first, count):
        for i in range(count):
            scores(first + i + 1, bufs[(i + 1) % 2][0])
            update(first + i, *bufs[i % 2], False)

    def block_group(u, carry):
        full_blocks(B_GROUP * u, B_GROUP)
        return carry

    lam = (jnp.exp(jnp.sum(lq1_ref[...] * lk1_ref[...], axis=-1, keepdims=True))
           - jnp.exp(jnp.sum(lq2_ref[...] * lk2_ref[...], axis=-1, keepdims=True)) + lambda_init)

    def query_block(j, carry):
        n_full = j
        begin(j)
        scores(0, s_a)
        lax.fori_loop(0, n_full // B_GROUP, block_group, 0)
        left = n_full % B_GROUP
        for count in range(B_GROUP):
            @pl.when(left == count)
            def _(count=count):
                full_blocks(n_full - count, count)
                update(n_full, *bufs[count % 2], True)

        inv_l = 1.0 / jnp.sum(l_ref[...], axis=-1, keepdims=True)
        o_all = acc_ref[...] * inv_l
        o = o_all[:B_ROWS] - lam * o_all[B_ROWS:]
        o_ref[0, key_rows(j), :] = (_rms_rows(o, gain_ref[...]) * (1.0 - lambda_init)).astype(BF16)
        return carry

    lax.fori_loop(0, seq // B_ROWS, query_block, 0)


def _diff_attn_call(q, k, v, lq1, lk1, lq2, lk2, gain, lambda_init):
    batch, seq, _ = q.shape
    rows = 2 * B_ROWS
    seq_spec = pl.BlockSpec((1, seq, LANES), lambda b, h: (b, 0, h))
    vec = lambda w: pl.BlockSpec((1, w), lambda b, h: (0, 0))
    return pl.pallas_call(
        functools.partial(_diff_attn_kernel, lambda_init),
        grid=(batch, N_DIFF_HEADS),
        in_specs=[seq_spec, seq_spec, seq_spec, vec(HEAD_DIM), vec(HEAD_DIM), vec(HEAD_DIM),
                  vec(HEAD_DIM), vec(2 * HEAD_DIM)],
        out_specs=seq_spec,
        out_shape=jax.ShapeDtypeStruct((batch, seq, SELF_WIDTH), BF16),
        scratch_shapes=[pltpu.VMEM((rows, LANES), BF16),
                        pltpu.VMEM((rows, B_ROWS), F32), pltpu.VMEM((rows, B_ROWS), F32),
                        pltpu.VMEM((rows, B_ROWS), BF16), pltpu.VMEM((rows, B_ROWS), BF16),
                        pltpu.VMEM((rows, 1), F32), pltpu.VMEM((rows, LANES), F32),
                        pltpu.VMEM((rows, LANES), F32)],
        compiler_params=pltpu.CompilerParams(dimension_semantics=("arbitrary", "arbitrary")),
        name="diff_attn",
    )(q, k, v, lq1, lk1, lq2, lk2, gain)


def _rope_tables(positions):
    inv_freq = 1.0 / (ROPE_THETA ** (jnp.arange(0, HEAD_DIM, 2, dtype=F32) / HEAD_DIM))
    ang = positions.astype(F32).reshape(-1, 1) * inv_freq
    cos, sin = jnp.cos(ang), jnp.sin(ang)
    reps = LANES // HEAD_DIM
    return (jnp.tile(jnp.concatenate([cos, cos], axis=1), (1, reps)),
            jnp.tile(jnp.concatenate([-sin, sin], axis=1), (1, reps)))


def kernel(x, mem, positions, ffn1_norm, ffn1_w_gate, ffn1_w_up, ffn1_w_down, mix_norm, mem_norm, w_in, w_mem_kv, mem_q_norm, mem_k_norm, w_out, a_q_norm, a_k_norm, a_rel_bias, b_q_norm, b_k_norm, b_lambda_q1, b_lambda_k1, b_lambda_q2, b_lambda_k2, b_subln, ffn2_norm, ffn2_w_gate, ffn2_w_up, ffn2_w_down):
    batch, seq, _ = x.shape
    tokens = batch * seq
    row = lambda a: a.reshape(1, -1).astype(F32)
    tiled = lambda a, heads: jnp.tile(a.astype(F32), heads).reshape(1, -1)
    bf = lambda a: a.astype(BF16)

    seg = jnp.arange(MXU_COLS) // HEAD_DIM
    seg_ones = (seg[:, None] == seg[None, :]).astype(BF16)
    rope = _rope_tables(positions)

    gk_mem = jnp.stack([tiled(mem_k_norm[i], N_MEM_HEADS) for i in range(DEPTH)])
    mem_k, mem_v = _mem_kv_call(mem, mem_norm.reshape(DEPTH, 1, D_MODEL), bf(w_mem_kv), gk_mem, seg_ones)

    stacked_gain = lambda a: a.reshape(DEPTH, 1, D_MODEL).astype(F32)
    ffn1 = (stacked_gain(ffn1_norm), bf(ffn1_w_gate), bf(ffn1_w_up), bf(ffn1_w_down))
    ffn2 = (stacked_gain(ffn2_norm), bf(ffn2_w_gate), bf(ffn2_w_up), bf(ffn2_w_down))
    mix_gain, w_in_bf, w_out_bf = stacked_gain(mix_norm), bf(w_in), bf(w_out)

    xf = x.reshape(tokens, D_MODEL)
    for i in range(DEPTH):
        j = i // N_MIXERS
        mixer_a = i % N_MIXERS == 0
        xf = _ffn_call(i, xf, *ffn1)
        gq, gk = (a_q_norm[j], a_k_norm[j]) if mixer_a else (b_q_norm[j], b_k_norm[j])
        q, k, v, o_m = _proj_call(i, xf, mix_gain, w_in_bf, tiled(gq, N_SELF_HEADS),
                                  tiled(gk, N_SELF_HEADS), tiled(mem_q_norm[i], N_MEM_HEADS),
                                  seg_ones, mem_k, mem_v, None if mixer_a else rope, seq)
        q, k, v = (a.reshape(batch, seq, SELF_WIDTH) for a in (q, k, v))
        if mixer_a:
            o_s = _chunk_attn_call(q, k, v, _chunk_bias_ramp(a_rel_bias[j]))
        else:
            lambda_init = 0.8 - 0.6 * math.exp(-0.3 * i)
            o_s = _diff_attn_call(q, k, v, row(b_lambda_q1[j]), row(b_lambda_k1[j]), row(b_lambda_q2[j]),
                                  row(b_lambda_k2[j]), row(b_subln[j]), lambda_init)
        xf = _ffn_call(i, xf, *ffn2, mix=(o_s.reshape(tokens, SELF_WIDTH), o_m, w_out_bf))
    return xf.reshape(batch, seq, D_MODEL)
```

```python
import functools
import math

import jax
import jax.numpy as jnp
from jax import lax
from jax.experimental import pallas as pl
from jax.experimental.pallas import tpu as pltpu

D_MODEL = 1024
DEPTH = 4
CHUNK = 64
HEAD_DIM = 64
N_SELF_HEADS = 12
N_DIFF_HEADS = 6
N_MEM_HEADS = 4
N_MEM = 256
SELF_WIDTH = N_SELF_HEADS * HEAD_DIM
MEM_WIDTH = N_MEM_HEADS * HEAD_DIM
IN_WIDTH = 3 * SELF_WIDTH + MEM_WIDTH
LEFT_CHUNKS = 8
REL_CLIP = 128
D_FF = 2816
ROPE_THETA = 10000.0
EPS = 1e-6
NEG_INF = -1e30
N_MIXERS = 2

LANES = 128
MXU_COLS = 256
VMEM_LIMIT = 56 * 1024 * 1024

FFN_ROWS = 512
PROJ_ROWS = 512
A_ROWS = 4 * CHUNK
A_PAD = LEFT_CHUNKS * CHUNK
A_BAND = A_ROWS + A_PAD
A_RAMP = A_BAND + A_ROWS
A_SUB = 128
B_ROWS = 512
B_SUB = 64
B_GROUP = 2

F32 = jnp.float32
BF16 = jnp.bfloat16


def _dot(a, b):
    return jnp.dot(a, b, preferred_element_type=F32)


def _dot_nt(a, b):
    return lax.dot_general(a, b, (((1,), (1,)), ((), ())), preferred_element_type=F32)


def _rms_rows(x, gain):
    ms = jnp.mean(x * x, axis=-1, keepdims=True)
    return x * lax.rsqrt(ms + EPS) * gain


def _head_rms(x, gain, seg_ones):
    x2 = x * x
    hi = x2.astype(BF16)
    lo = (x2 - hi.astype(F32)).astype(BF16)
    parts = []
    for g in range(x.shape[1] // MXU_COLS):
        sl = slice(g * MXU_COLS, (g + 1) * MXU_COLS)
        parts.append(_dot(hi[:, sl], seg_ones) + _dot(lo[:, sl], seg_ones))
    ss = parts[0] if len(parts) == 1 else jnp.concatenate(parts, axis=1)
    return x * lax.rsqrt(ss * (1.0 / HEAD_DIM) + EPS) * gain


def _rope(x, cos_t, sin_t):
    half = HEAD_DIM // 2
    lane = lax.broadcasted_iota(jnp.int32, cos_t.shape, 1)
    first_half = (lane % HEAD_DIM) < half
    tiles = []
    for g in range(x.shape[1] // LANES):
        xg = x[:, g * LANES:(g + 1) * LANES]
        fwd = pltpu.roll(xg, LANES - half, axis=1)
        bwd = pltpu.roll(xg, half, axis=1)
        tiles.append(xg * cos_t + jnp.where(first_half, fwd, bwd) * sin_t)
    return jnp.concatenate(tiles, axis=1)


def _swiglu_residual(x, gain, wg_ref, wu_ref, wd_ref):
    h = _rms_rows(x, gain).astype(BF16)
    gate = _dot(h, wg_ref[...])
    up = _dot(h, wu_ref[...])
    act = (gate * (1.0 / (1.0 + jnp.exp(-gate))) * up).astype(BF16)
    return x + 0.5 * _dot(act, wd_ref[...])


def _ffn_kernel(x_ref, g_ref, wg_ref, wu_ref, wd_ref, o_ref):
    o_ref[...] = _swiglu_residual(x_ref[...], g_ref[...], wg_ref, wu_ref, wd_ref)


def _out_ffn_kernel(x_ref, os_ref, om_ref, wos_ref, wom_ref, g_ref, wg_ref, wu_ref, wd_ref, o_ref):
    x = x_ref[...] + _dot(os_ref[...], wos_ref[...]) + _dot(om_ref[...], wom_ref[...])
    o_ref[...] = _swiglu_residual(x, g_ref[...], wg_ref, wu_ref, wd_ref)


def _resident(shape):
    return pl.BlockSpec(shape, lambda i: (0,) * len(shape), pipeline_mode=pl.Buffered(1))


def _layer_resident(layer, shape, row_block=0):
    return pl.BlockSpec((None,) + shape, lambda i: (layer, row_block) + (0,) * (len(shape) - 1),
                        pipeline_mode=pl.Buffered(1))


def _ffn_call(layer, x, gain, wg, wu, wd, mix=None):
    tokens = x.shape[0]
    rows = pl.BlockSpec((FFN_ROWS, D_MODEL), lambda i: (i, 0))
    ffn_specs = [_layer_resident(layer, (1, D_MODEL)), _layer_resident(layer, (D_MODEL, D_FF)),
                 _layer_resident(layer, (D_MODEL, D_FF)), _layer_resident(layer, (D_FF, D_MODEL))]
    if mix is None:
        body, ins, specs, name = _ffn_kernel, (x, gain, wg, wu, wd), [rows] + ffn_specs, "ffn"
    else:
        o_s, o_m, w_out = mix
        body, name = _out_ffn_kernel, "out_ffn"
        ins = (x, o_s, o_m, w_out, w_out, gain, wg, wu, wd)
        specs = [rows,
                 pl.BlockSpec((FFN_ROWS, SELF_WIDTH), lambda i: (i, 0)),
                 pl.BlockSpec((FFN_ROWS, MEM_WIDTH), lambda i: (i, 0)),
                 _layer_resident(layer, (SELF_WIDTH, D_MODEL)),
                 _layer_resident(layer, (MEM_WIDTH, D_MODEL), SELF_WIDTH // MEM_WIDTH)] + ffn_specs
    return pl.pallas_call(
        body,
        grid=(tokens // FFN_ROWS,),
        in_specs=specs,
        out_specs=rows,
        out_shape=jax.ShapeDtypeStruct((tokens, D_MODEL), F32),
        compiler_params=pltpu.CompilerParams(dimension_semantics=("arbitrary",),
                                             vmem_limit_bytes=VMEM_LIMIT),
        name=name,
    )(*ins)


def _mem_kv_kernel(mem_ref, g_ref, w_ref, gk_ref, ones_ref, k_ref, v_ref):
    h = _rms_rows(mem_ref[0], g_ref[0]).astype(BF16)
    kv = _dot(h, w_ref[0])
    k_ref[0, 0] = _head_rms(kv[:, :MEM_WIDTH], gk_ref[0], ones_ref[...]).astype(BF16)
    v_ref[0, 0] = kv[:, MEM_WIDTH:].astype(BF16)


def _mem_kv_call(mem, mem_norm, w_mem_kv, gk_tiled, seg_ones):
    batch = mem.shape[0]
    out = jax.ShapeDtypeStruct((DEPTH, batch, N_MEM, MEM_WIDTH), BF16)
    kv_spec = pl.BlockSpec((1, 1, N_MEM, MEM_WIDTH), lambda l, b: (l, b, 0, 0))
    return pl.pallas_call(
        _mem_kv_kernel,
        grid=(DEPTH, batch),
        in_specs=[pl.BlockSpec((1, N_MEM, D_MODEL), lambda l, b: (b, 0, 0)),
                  pl.BlockSpec((1, 1, D_MODEL), lambda l, b: (l, 0, 0)),
                  pl.BlockSpec((1, D_MODEL, 2 * MEM_WIDTH), lambda l, b: (l, 0, 0)),
                  pl.BlockSpec((1, 1, MEM_WIDTH), lambda l, b: (l, 0, 0)),
                  pl.BlockSpec((MXU_COLS, MXU_COLS), lambda l, b: (0, 0))],
        out_specs=[kv_spec, kv_spec],
        out_shape=[out, out],
        compiler_params=pltpu.CompilerParams(dimension_semantics=("arbitrary", "arbitrary")),
        name="mem_kv",
    )(mem, mem_norm, w_mem_kv, gk_tiled, seg_ones)


def _proj_kernel(use_rope, x_ref, g_ref, w_ref, gq_ref, gk_ref, gm_ref, ones_ref, km_ref, vm_ref,
                 *rest):
    if use_rope:
        cos_ref, sin_ref, q_ref, k_ref, v_ref, om_ref = rest
    else:
        q_ref, k_ref, v_ref, om_ref = rest
    seg_ones = ones_ref[...]
    h = _rms_rows(x_ref[...], g_ref[...]).astype(BF16)

    def section(first, width):
        return _dot(h, w_ref[:, first:first + width])

    scale = HEAD_DIM ** -0.5
    q = _head_rms(section(0, SELF_WIDTH), gq_ref[...], seg_ones)
    if use_rope:
        q = _rope(q, cos_ref[...], sin_ref[...])
    q_ref[...] = (q * scale).astype(BF16)
    k = _head_rms(section(SELF_WIDTH, SELF_WIDTH), gk_ref[...], seg_ones)
    if use_rope:
        k = _rope(k, cos_ref[...], sin_ref[...])
    k_ref[...] = k.astype(BF16)
    v_ref[...] = section(2 * SELF_WIDTH, SELF_WIDTH).astype(BF16)

    qm = (_head_rms(section(3 * SELF_WIDTH, MEM_WIDTH), gm_ref[...], seg_ones) * scale).astype(BF16)
    km = km_ref[0, 0]
    vm = vm_ref[0, 0]
    head_of_lane = lax.broadcasted_iota(jnp.int32, qm.shape, 1) // HEAD_DIM
    om = jnp.zeros(qm.shape, F32)
    for hd in range(N_MEM_HEADS):
        mine = head_of_lane == hd
        s = _dot_nt(jnp.where(mine, qm, jnp.zeros_like(qm)), km)
        e = jnp.exp(s - jnp.max(s, axis=-1, keepdims=True))
        inv = 1.0 / jnp.sum(e, axis=-1, keepdims=True)
        om = jnp.where(mine, _dot(e.astype(BF16), vm) * inv, om)
    om_ref[...] = om.astype(BF16)


def _proj_call(layer, x, gain, w_in, gq, gk, gm, seg_ones, mem_k, mem_v, rope, seq):
    tokens = x.shape[0]
    per_seq = seq // PROJ_ROWS
    rows = lambda w: pl.BlockSpec((PROJ_ROWS, w), lambda i: (i, 0))
    mem_spec = pl.BlockSpec((1, 1, N_MEM, MEM_WIDTH), lambda i: (layer, i // per_seq, 0, 0))
    specs = [rows(D_MODEL), _layer_resident(layer, (1, D_MODEL)), _layer_resident(layer, (D_MODEL, IN_WIDTH)),
             _resident((1, SELF_WIDTH)), _resident((1, SELF_WIDTH)), _resident((1, MEM_WIDTH)),
             _resident((MXU_COLS, MXU_COLS)), mem_spec, mem_spec]
    ins = [x, gain, w_in, gq, gk, gm, seg_ones, mem_k, mem_v]
    if rope is not None:
        specs += [rows(LANES), rows(LANES)]
        ins += list(rope)
    wide = jax.ShapeDtypeStruct((tokens, SELF_WIDTH), BF16)
    return pl.pallas_call(
        functools.partial(_proj_kernel, rope is not None),
        grid=(tokens // PROJ_ROWS,),
        in_specs=specs,
        out_specs=[rows(SELF_WIDTH), rows(SELF_WIDTH), rows(SELF_WIDTH), rows(MEM_WIDTH)],
        out_shape=[wide, wide, wide, jax.ShapeDtypeStruct((tokens, MEM_WIDTH), BF16)],
        compiler_params=pltpu.CompilerParams(dimension_semantics=("arbitrary",),
                                             vmem_limit_bytes=VMEM_LIMIT),
        name="proj_rope" if rope is not None else "proj",
    )(*ins)


def _chunk_attn_kernel(q_ref, k_ref, v_ref, ramp_ref, o_ref, tab_ref, kpad, vpad, s_a, s_b, p_a, p_b):
    seq = q_ref.shape[1]
    n_blocks = seq // A_ROWS
    n_head = A_PAD // A_ROWS
    rows = 2 * A_ROWS

    row_chunk = lax.broadcasted_iota(jnp.int32, (A_ROWS, A_BAND), 0) // CHUNK
    col_chunk = lax.broadcasted_iota(jnp.int32, (A_ROWS, A_BAND), 1) // CHUNK
    allowed = (col_chunk >= row_chunk) & (col_chunk <= row_chunk + LEFT_CHUNKS)
    for hd in range(LANES // HEAD_DIM):
        ramp = jnp.broadcast_to(ramp_ref[hd], (A_ROWS, A_RAMP))
        toeplitz = pltpu.roll(ramp, A_RAMP - A_ROWS, axis=1, stride=1, stride_axis=0)
        tab_ref[hd * A_ROWS:(hd + 1) * A_ROWS, :] = jnp.where(allowed, toeplitz[:, :A_BAND], NEG_INF)

    kpad[:A_PAD, :] = jnp.zeros((A_PAD, LANES), BF16)
    vpad[:A_PAD, :] = jnp.zeros((A_PAD, LANES), BF16)
    kpad[A_PAD:, :] = k_ref[0, :n_head * A_ROWS, :]
    vpad[A_PAD:, :] = v_ref[0, :n_head * A_ROWS, :]

    def band(ref, pad_ref, t):
        if isinstance(t, int) and t < n_head:
            return pad_ref[t * A_ROWS:t * A_ROWS + A_BAND, :]
        return ref[0, pl.ds(pl.multiple_of((t - n_head) * A_ROWS, A_ROWS), A_BAND), :]

    def block_rows(t):
        return pl.ds(pl.multiple_of(t * A_ROWS, A_ROWS), A_ROWS)

    lane = lax.broadcasted_iota(jnp.int32, (A_ROWS, LANES), 1)

    def scores(t, s_ref):
        q = q_ref[0, block_rows(t), :]
        stacked = jnp.concatenate([jnp.where(lane < HEAD_DIM, q, jnp.zeros_like(q)),
                                   jnp.where(lane >= HEAD_DIM, q, jnp.zeros_like(q))], axis=0)
        s_ref[...] = _dot_nt(stacked, band(k_ref, kpad, t))

    def update(t, s_ref, p_ref, first_slot):
        inv = []
        for c in range(rows // A_SUB):
            r = slice(c * A_SUB, (c + 1) * A_SUB)
            first_chunk = ((c * A_SUB) % A_ROWS) // CHUNK
            last_chunk = ((c * A_SUB) % A_ROWS + A_SUB - 1) // CHUNK
            lo = first_chunk * CHUNK // LANES * LANES
            hi = -(-((last_chunk + LEFT_CHUNKS + 1) * CHUNK) // LANES) * LANES
            s = s_ref[r, lo:hi] + tab_ref[r, lo:hi]
            if first_slot:
                slot = lax.broadcasted_iota(jnp.int32, s.shape, 1) + lo
                s = jnp.where(slot >= first_slot, s, NEG_INF)
            e = jnp.exp(s - jnp.max(s, axis=-1, keepdims=True))
            part = e[:, :LANES]
            for g in range(1, (hi - lo) // LANES):
                part = part + e[:, g * LANES:(g + 1) * LANES]
            inv.append(1.0 / jnp.sum(part, axis=-1, keepdims=True))
            p_ref[r, lo:hi] = e.astype(BF16)
            if lo > 0:
                p_ref[r, :lo] = jnp.zeros((A_SUB, lo), BF16)
            if hi < A_BAND:
                p_ref[r, hi:] = jnp.zeros((A_SUB, A_BAND - hi), BF16)
        pv = _dot(p_ref[...], band(v_ref, vpad, t)) * jnp.concatenate(inv, axis=0)
        o_ref[0, block_rows(t), :] = jnp.where(lane < HEAD_DIM, pv[:A_ROWS], pv[A_ROWS:]).astype(BF16)

    scores(0, s_a)
    scores(1, s_b)
    update(0, s_a, p_a, A_PAD)
    scores(2, s_a)
    update(1, s_b, p_b, A_PAD - A_ROWS)

    def two_blocks(u, carry):
        t = 2 * u
        scores(t + 1, s_b)
        update(t, s_a, p_a, 0)
        scores(jnp.minimum(t + 2, n_blocks - 1), s_a)
        update(t + 1, s_b, p_b, 0)
        return carry

    lax.fori_loop(1, n_blocks // 2, two_blocks, 0)


def _chunk_attn_call(q, k, v, ramp):
    batch, seq, _ = q.shape
    rows = 2 * A_ROWS
    heads_per = LANES // HEAD_DIM
    n_head = A_PAD // A_ROWS
    seq_spec = pl.BlockSpec((1, seq, LANES), lambda b, p: (b, 0, p))
    return pl.pallas_call(
        _chunk_attn_kernel,
        grid=(batch, SELF_WIDTH // LANES),
        in_specs=[seq_spec, seq_spec, seq_spec,
                  pl.BlockSpec((heads_per, 1, A_RAMP), lambda b, p: (p, 0, 0))],
        out_specs=seq_spec,
        out_shape=jax.ShapeDtypeStruct((batch, seq, SELF_WIDTH), BF16),
        scratch_shapes=[pltpu.VMEM((rows, A_BAND), F32),
                        pltpu.VMEM((A_PAD + n_head * A_ROWS, LANES), BF16),
                        pltpu.VMEM((A_PAD + n_head * A_ROWS, LANES), BF16),
                        pltpu.VMEM((rows, A_BAND), F32), pltpu.VMEM((rows, A_BAND), F32),
                        pltpu.VMEM((rows, A_BAND), BF16), pltpu.VMEM((rows, A_BAND), BF16)],
        compiler_params=pltpu.CompilerParams(dimension_semantics=("arbitrary", "arbitrary")),
        name="chunk_attn",
    )(q, k, v, ramp)


def _chunk_bias_ramp(rel_bias):
    heads = rel_bias.shape[0]
    n_far = A_PAD + A_ROWS - REL_CLIP
    n_near = A_RAMP - n_far - (2 * REL_CLIP + 1)
    ramp = jnp.concatenate([jnp.broadcast_to(rel_bias[:, -1:], (heads, n_far)), rel_bias[:, ::-1],
                            jnp.broadcast_to(rel_bias[:, :1], (heads, n_near))], axis=1)
    return ramp.astype(F32).reshape(heads, 1, A_RAMP)


def _diff_attn_kernel(lambda_init, q_ref, k_ref, v_ref, lq1_ref, lk1_ref, lq2_ref, lk2_ref,
                      gain_ref, o_ref, qs_ref, s_a, s_b, p_a, p_b, m_ref, l_ref, acc_ref):
    rows = 2 * B_ROWS
    seq = q_ref.shape[1]

    def begin(j):
        q = q_ref[0, key_rows(j), :]
        lane = lax.broadcasted_iota(jnp.int32, q.shape, 1)
        qs_ref[:B_ROWS, :] = jnp.where(lane < HEAD_DIM, q, jnp.zeros_like(q))
        qs_ref[B_ROWS:, :] = jnp.where(lane >= HEAD_DIM, q, jnp.zeros_like(q))
        m_ref[...] = jnp.full(m_ref.shape, NEG_INF, F32)
        l_ref[...] = jnp.zeros(l_ref.shape, F32)
        acc_ref[...] = jnp.zeros(acc_ref.shape, F32)

    def key_rows(t):
        return pl.ds(pl.multiple_of(t * B_ROWS, B_ROWS), B_ROWS)

    def scores(t, s_ref):
        s_ref[...] = _dot_nt(qs_ref[...], k_ref[0, key_rows(t), :])

    def update(t, s_ref, p_ref, diagonal):
        for c in range(rows // B_SUB):
            r = slice(c * B_SUB, (c + 1) * B_SUB)
            s = s_ref[r, :]
            if diagonal:
                base = (c * B_SUB) % B_ROWS
                row_chunk = (lax.broadcasted_iota(jnp.int32, s.shape, 0) + base) // CHUNK
                col_chunk = lax.broadcasted_iota(jnp.int32, s.shape, 1) // CHUNK
                s = jnp.where(col_chunk <= row_chunk, s, NEG_INF)
            m_old = m_ref[r, :]
            m_new = jnp.maximum(m_old, jnp.max(s, axis=-1, keepdims=True))
            alpha = jnp.exp(m_old - m_new)
            e = jnp.exp(s - m_new)
            part = e[:, :LANES]
            for g in range(1, B_ROWS // LANES):
                part = part + e[:, g * LANES:(g + 1) * LANES]
            m_ref[r, :] = m_new
            l_ref[r, :] = alpha * l_ref[r, :] + part
            acc_ref[r, :] = alpha * acc_ref[r, :]
            p_ref[r, :] = e.astype(BF16)
        acc_ref[...] += _dot(p_ref[...], v_ref[0, key_rows(t), :])

    bufs = ((s_a, p_a), (s_b, p_b))

    def full_blocks(first, count):
        for i in range(count):
            scores(first + i + 1, bufs[(i + 1) % 2][0])
            update(first + i, *bufs[i % 2], False)

    def block_group(u, carry):
        full_blocks(B_GROUP * u, B_GROUP)
        return carry

    lam = (jnp.exp(jnp.sum(lq1_ref[...] * lk1_ref[...], axis=-1, keepdims=True))
           - jnp.exp(jnp.sum(lq2_ref[...] * lk2_ref[...], axis=-1, keepdims=True)) + lambda_init)

    def query_block(j, carry):
        n_full = j
        begin(j)
        scores(0, s_a)
        lax.fori_loop(0, n_full // B_GROUP, block_group, 0)
        left = n_full % B_GROUP
        for count in range(B_GROUP):
            @pl.when(left == count)
            def _(count=count):
                full_blocks(n_full - count, count)
                update(n_full, *bufs[count % 2], True)

        inv_l = 1.0 / jnp.sum(l_ref[...], axis=-1, keepdims=True)
        o_all = acc_ref[...] * inv_l
        o = o_all[:B_ROWS] - lam * o_all[B_ROWS:]
        o_ref[0, key_rows(j), :] = (_rms_rows(o, gain_ref[...]) * (1.0 - lambda_init)).astype(BF16)
        return carry

    lax.fori_loop(0, seq // B_ROWS, query_block, 0)


def _diff_attn_call(q, k, v, lq1, lk1, lq2, lk2, gain, lambda_init):
    batch, seq, _ = q.shape
    rows = 2 * B_ROWS
    seq_spec = pl.BlockSpec((1, seq, LANES), lambda b, h: (b, 0, h))
    vec = lambda w: pl.BlockSpec((1, w), lambda b, h: (0, 0))
    return pl.pallas_call(
        functools.partial(_diff_attn_kernel, lambda_init),
        grid=(batch, N_DIFF_HEADS),
        in_specs=[seq_spec, seq_spec, seq_spec, vec(HEAD_DIM), vec(HEAD_DIM), vec(HEAD_DIM),
                  vec(HEAD_DIM), vec(2 * HEAD_DIM)],
        out_specs=seq_spec,
        out_shape=jax.ShapeDtypeStruct((batch, seq, SELF_WIDTH), BF16),
        scratch_shapes=[pltpu.VMEM((rows, LANES), BF16),
                        pltpu.VMEM((rows, B_ROWS), F32), pltpu.VMEM((rows, B_ROWS), F32),
                        pltpu.VMEM((rows, B_ROWS), BF16), pltpu.VMEM((rows, B_ROWS), BF16),
                        pltpu.VMEM((rows, 1), F32), pltpu.VMEM((rows, LANES), F32),
                        pltpu.VMEM((rows, LANES), F32)],
        compiler_params=pltpu.CompilerParams(dimension_semantics=("arbitrary", "arbitrary")),
        name="diff_attn",
    )(q, k, v, lq1, lk1, lq2, lk2, gain)


def _rope_tables(positions):
    inv_freq = 1.0 / (ROPE_THETA ** (jnp.arange(0, HEAD_DIM, 2, dtype=F32) / HEAD_DIM))
    ang = positions.astype(F32).reshape(-1, 1) * inv_freq
    cos, sin = jnp.cos(ang), jnp.sin(ang)
    reps = LANES // HEAD_DIM
    return (jnp.tile(jnp.concatenate([cos, cos], axis=1), (1, reps)),
            jnp.tile(jnp.concatenate([-sin, sin], axis=1), (1, reps)))


def kernel(x, mem, positions, ffn1_norm, ffn1_w_gate, ffn1_w_up, ffn1_w_down, mix_norm, mem_norm, w_in, w_mem_kv, mem_q_norm, mem_k_norm, w_out, a_q_norm, a_k_norm, a_rel_bias, b_q_norm, b_k_norm, b_lambda_q1, b_lambda_k1, b_lambda_q2, b_lambda_k2, b_subln, ffn2_norm, ffn2_w_gate, ffn2_w_up, ffn2_w_down):
    batch, seq, _ = x.shape
    tokens = batch * seq
    row = lambda a: a.reshape(1, -1).astype(F32)
    tiled = lambda a, heads: jnp.tile(a.astype(F32), heads).reshape(1, -1)
    bf = lambda a: a.astype(BF16)

    seg = jnp.arange(MXU_COLS) // HEAD_DIM
    seg_ones = (seg[:, None] == seg[None, :]).astype(BF16)
    rope = _rope_tables(positions)

    gk_mem = jnp.stack([tiled(mem_k_norm[i], N_MEM_HEADS) for i in range(DEPTH)])
    mem_k, mem_v = _mem_kv_call(mem, mem_norm.reshape(DEPTH, 1, D_MODEL), bf(w_mem_kv), gk_mem, seg_ones)

    stacked_gain = lambda a: a.reshape(DEPTH, 1, D_MODEL).astype(F32)
    ffn1 = (stacked_gain(ffn1_norm), bf(ffn1_w_gate), bf(ffn1_w_up), bf(ffn1_w_down))
    ffn2 = (stacked_gain(ffn2_norm), bf(ffn2_w_gate), bf(ffn2_w_up), bf(ffn2_w_down))
    mix_gain, w_in_bf, w_out_bf = stacked_gain(mix_norm), bf(w_in), bf(w_out)

    xf = x.reshape(tokens, D_MODEL)
    for i in range(DEPTH):
        j = i // N_MIXERS
        mixer_a = i % N_MIXERS == 0
        xf = _ffn_call(i, xf, *ffn1)
        gq, gk = (a_q_norm[j], a_k_norm[j]) if mixer_a else (b_q_norm[j], b_k_norm[j])
        q, k, v, o_m = _proj_call(i, xf, mix_gain, w_in_bf, tiled(gq, N_SELF_HEADS),
                                  tiled(gk, N_SELF_HEADS), tiled(mem_q_norm[i], N_MEM_HEADS),
                                  seg_ones, mem_k, mem_v, None if mixer_a else rope, seq)
        q, k, v = (a.reshape(batch, seq, SELF_WIDTH) for a in (q, k, v))
        if mixer_a:
            o_s = _chunk_attn_call(q, k, v, _chunk_bias_ramp(a_rel_bias[j]))
        else:
            lambda_init = 0.8 - 0.6 * math.exp(-0.3 * i)
            o_s = _diff_attn_call(q, k, v, row(b_lambda_q1[j]), row(b_lambda_k1[j]), row(b_lambda_q2[j]),
                                  row(b_lambda_k2[j]), row(b_subln[j]), lambda_init)
        xf = _ffn_call(i, xf, *ffn2, mix=(o_s.reshape(tokens, SELF_WIDTH), o_m, w_out_bf))
    return xf.reshape(batch, seq, D_MODEL)
```
